```python
import jax, jax.numpy as jnp
from jax import lax
import numpy as np

D_MODEL = 2048
BATCH = 2
SEQ = 8192
DEPTH = 1

N_ATTN_HEADS = 8
HEAD_DIM = 128
D_ATTN = N_ATTN_HEADS * HEAD_DIM
D_CONV = D_MODEL - D_ATTN
D_MIX = D_ATTN + D_CONV
D_IN = 3 * D_ATTN + 2 * D_CONV
DILATED_BRANCHES = ((128, 1), (512, 4), (2048, 16))
BLK = 128
CONV_WIDTH = 31
D_FF = ((-(-8 * D_MODEL // 3) + 255) // 256) * 256
D_PLE = 256
EPS = 1e-6

kernel_name = "hymba_dilated_attn_conformer_conv_hybrid"


def rmsnorm(x, g):
    xf = x.astype(jnp.float32)
    y = xf * lax.rsqrt(jnp.mean(xf * xf, axis=-1, keepdims=True) + EPS)
    return (y * g.astype(jnp.float32)).astype(x.dtype)


def layernorm(x, g, b):
    xf = x.astype(jnp.float32)
    mu = jnp.mean(xf, axis=-1, keepdims=True)
    var = jnp.mean(jnp.square(xf - mu), axis=-1, keepdims=True)
    y = (xf - mu) * lax.rsqrt(var + EPS)
    return (y * g.astype(jnp.float32) + b.astype(jnp.float32)).astype(x.dtype)


def dilated_branch(q, k, v, window, dilation):
    B, S, H, Dh = q.shape
    span = dilation * BLK
    L = -(-S // span) * span
    nb = L // span
    pad = ((0, 0), (0, L - S), (0, 0), (0, 0))

    def blocks(a):
        return jnp.pad(a, pad).reshape(B, nb, BLK, dilation, H, Dh)

    def with_prev(a):
        prev = jnp.pad(a, ((0, 0), (1, 0), (0, 0), (0, 0), (0, 0), (0, 0)))[:, :-1]
        return jnp.concatenate([prev, a], axis=2)

    qb = blocks(q)
    kk = with_prev(blocks(k))
    vv = with_prev(blocks(v))
    s = jnp.einsum('bnqrhd,bnkrhd->bnrhqk', qb, kk).astype(jnp.float32) * (Dh ** -0.5)
    qi = jnp.arange(BLK)[:, None]
    kj = jnp.arange(2 * BLK)[None, :]
    dist = qi + BLK - kj
    band = (dist >= 0) & (dist <= window // dilation)
    first = (jnp.arange(nb)[:, None, None] > 0) | (kj[None] >= BLK)
    mask = band[None] & first
    s = jnp.where(mask[None, :, None, None], s, -jnp.inf)
    m = jnp.max(s, axis=-1, keepdims=True)
    e = jnp.exp(s - m)
    den = jnp.sum(e, axis=-1, keepdims=True)
    o = jnp.einsum('bnrhqk,bnkrhd->bnqrhd', (e / den).astype(v.dtype), vv)
    o = o.reshape(B, L, H, Dh)[:, :S]
    lse = (m + jnp.log(den))[..., 0]
    lse = lse.transpose(0, 1, 4, 2, 3).reshape(B, L, H)[:, :S]
    return o, lse


def dilated_attention(q, k, v):
    outs, lses = [], []
    for window, dilation in DILATED_BRANCHES:
        o, lse = dilated_branch(q, k, v, window, dilation)
        outs.append(o)
        lses.append(lse)
    w = jax.nn.softmax(jnp.stack(lses, axis=0), axis=0)
    o = jnp.sum(w[..., None] * jnp.stack(outs, axis=0).astype(jnp.float32), axis=0)
    return o.astype(q.dtype)


def conformer_conv(cv, cg, w_dw, b_dw, g_ln, b_ln):
    u = cv * jax.nn.sigmoid(cg)
    u = lax.conv_general_dilated(u, w_dw.astype(u.dtype), window_strides=(1,),
                                 padding=[(CONV_WIDTH - 1, 0)],
                                 dimension_numbers=('NWC', 'WIO', 'NWC'),
                                 feature_group_count=D_CONV) + b_dw
    return jax.nn.silu(layernorm(u, g_ln, b_ln))


def setup_inputs(seed: int = 0) -> dict:
    key = jax.random.key(seed)
    ks = jax.random.split(key, 20)

    def nrm(k, shape, scale):
        return jax.random.normal(k, shape, jnp.float32) * scale

    def gain(k, n):
        return 1.0 + 0.02 * jax.random.normal(k, (DEPTH, n), jnp.float32)

    return {
        "x": nrm(ks[0], (BATCH, SEQ, D_MODEL), 1.0),
        "p": nrm(ks[1], (DEPTH, BATCH, SEQ, D_PLE), 1.0),
        "g_mix": gain(ks[2], D_MODEL),
        "w_in": nrm(ks[3], (DEPTH, D_MODEL, D_IN), D_MODEL ** -0.5),
        "w_dw": nrm(ks[4], (DEPTH, CONV_WIDTH, 1, D_CONV), CONV_WIDTH ** -0.5),
        "b_dw": nrm(ks[5], (DEPTH, D_CONV), 0.02),
        "g_conv_ln": gain(ks[6], D_CONV),
        "b_conv_ln": nrm(ks[7], (DEPTH, D_CONV), 0.02),
        "w_out": nrm(ks[8], (DEPTH, D_MIX, D_MODEL), D_MIX ** -0.5),
        "g_ffn": gain(ks[9], D_MODEL),
        "w_gate": nrm(ks[10], (DEPTH, D_MODEL, D_FF), D_MODEL ** -0.5),
        "w_up": nrm(ks[11], (DEPTH, D_MODEL, D_FF), D_MODEL ** -0.5),
        "w_down": nrm(ks[12], (DEPTH, D_FF, D_MODEL), D_FF ** -0.5),
        "g_ple": gain(ks[13], D_MODEL),
        "w_pgate": nrm(ks[14], (DEPTH, D_MODEL, D_MODEL), D_MODEL ** -0.5),
        "b_pgate": nrm(ks[15], (DEPTH, D_MODEL), 0.02),
        "w_ple": nrm(ks[16], (DEPTH, D_PLE, D_MODEL), D_PLE ** -0.5),
        "g_final": 1.0 + 0.02 * jax.random.normal(ks[17], (D_MODEL,), jnp.float32),
    }


def reference(x, p, g_mix, w_in, w_dw, b_dw, g_conv_ln, b_conv_ln, w_out, g_ffn,
              w_gate, w_up, w_down, g_ple, w_pgate, b_pgate, w_ple, g_final):
    B, S, _ = x.shape
    h = x
    for i in range(DEPTH):
        a = rmsnorm(h, g_mix[i])
        z = a @ w_in[i]
        q, k, v, cv, cg = jnp.split(
            z, [D_ATTN, 2 * D_ATTN, 3 * D_ATTN, 3 * D_ATTN + D_CONV], axis=-1)
        hd = (B, S, N_ATTN_HEADS, HEAD_DIM)
        o_attn = dilated_attention(q.reshape(hd), k.reshape(hd), v.reshape(hd)).reshape(B, S, D_ATTN)
        o_conv = conformer_conv(cv, cg, w_dw[i], b_dw[i], g_conv_ln[i], b_conv_ln[i])
        h = h + jnp.concatenate([o_attn, o_conv], axis=-1) @ w_out[i]
        f = rmsnorm(h, g_ffn[i])
        h = h + (jax.nn.silu(f @ w_gate[i]) * (f @ w_up[i])) @ w_down[i]
        gte = jax.nn.sigmoid(rmsnorm(h, g_ple[i]) @ w_pgate[i] + b_pgate[i])
        h = h + (p[i] @ w_ple[i]) * gte
    return rmsnorm(h, g_final)
```

```python
import functools

import jax
import jax.numpy as jnp
from jax import lax
from jax.experimental import pallas as pl
from jax.experimental.pallas import tpu as pltpu

F32 = jnp.float32
BF16 = jnp.bfloat16

EPS = 1e-6
N_HEADS = 8
HEAD_DIM = 128
D_ATTN = N_HEADS * HEAD_DIM
BLK = 128
DILATIONS = (1, 4, 16)
CONV_WIDTH = 31
CHUNK = DILATIONS[-1] * BLK
NEG = -1e30

VMEM_LIMIT = 56 * 1024 * 1024


def _cparams(sem):
    return pltpu.CompilerParams(dimension_semantics=sem, vmem_limit_bytes=VMEM_LIMIT)


def _rms(x, g):
    return x * lax.rsqrt(jnp.mean(x * x, axis=-1, keepdims=True) + EPS) * g


def _in_proj_kernel(x_ref, g_ref, w_ref, o_ref, a_ref):
    @pl.when(pl.program_id(1) == 0)
    def _():
        a_ref[...] = _rms(x_ref[...], g_ref[...]).astype(BF16)

    o_ref[...] = jnp.dot(a_ref[...], w_ref[...], preferred_element_type=F32).astype(o_ref.dtype)


def _in_proj(x2, g, w, tm=1024, tn=1024):
    M, D = x2.shape
    N = w.shape[1]
    return pl.pallas_call(
        _in_proj_kernel,
        grid=(M // tm, N // tn),
        in_specs=[
            pl.BlockSpec((tm, D), lambda i, j: (i, 0)),
            pl.BlockSpec((1, D), lambda i, j: (0, 0)),
            pl.BlockSpec((D, tn), lambda i, j: (0, j)),
        ],
        out_specs=pl.BlockSpec((tm, tn), lambda i, j: (i, j)),
        out_shape=jax.ShapeDtypeStruct((M, N), BF16),
        scratch_shapes=[pltpu.VMEM((tm, D), BF16)],
        compiler_params=_cparams(("parallel", "arbitrary")),
        name="in_proj",
    )(x2, g, w)


def _attn_kernel(q_ref, k_ref, v_ref, o_ref, qf, kf, vf, num, mm, ll, bias, *, seq):
    pad = CHUNK
    scale = HEAD_DIM ** -0.5
    cp = 512

    kf[pl.ds(0, pad), :] = jnp.zeros((pad, HEAD_DIM), F32)
    vf[pl.ds(0, pad), :] = jnp.zeros((pad, HEAD_DIM), F32)

    def cast_body(i, _):
        r = pl.multiple_of(i * cp, cp)
        qf[pl.ds(r, cp), :] = q_ref[pl.ds(r, cp), :].astype(F32) * scale
        kf[pl.ds(pad + r, cp), :] = k_ref[pl.ds(r, cp), :].astype(F32)
        vf[pl.ds(pad + r, cp), :] = v_ref[pl.ds(r, cp), :].astype(F32)
        return 0

    lax.fori_loop(0, seq // cp, cast_body, 0)

    qi = lax.broadcasted_iota(jnp.int32, (BLK, 2 * BLK), 0)
    kj = lax.broadcasted_iota(jnp.int32, (BLK, 2 * BLK), 1)
    band = (kj >= qi) & (kj <= qi + BLK)
    bias[0] = jnp.where(band, 0.0, NEG).astype(F32)
    bias[1] = jnp.where(band & (kj >= BLK), 0.0, NEG).astype(F32)

    def chunk_body(c, _):
        base = pl.multiple_of(c * CHUNK, CHUNK)

        for bi, d in enumerate(DILATIONS):
            span = d * BLK

            def blk_body(idx, _, bi=bi, d=d, span=span):
                nb = idx // d
                r = idx % d
                loc = nb * span + r
                q0 = base + loc
                k0 = pad + q0 - span
                if d == 1:
                    qs = pl.ds(q0, BLK)
                    ks = pl.ds(k0, 2 * BLK)
                    os_ = pl.ds(loc, BLK)
                else:
                    qs = pl.ds(q0, BLK, stride=d)
                    ks = pl.ds(k0, 2 * BLK, stride=d)
                    os_ = pl.ds(loc, BLK, stride=d)
                q = qf[qs, :].astype(BF16)
                k = kf[ks, :].astype(BF16)
                v = vf[ks, :].astype(BF16)
                s = lax.dot_general(q, k, (((1,), (1,)), ((), ())), preferred_element_type=F32)
                first = (base + nb * span == 0).astype(jnp.int32)
                s = s + bias[first]
                m = jnp.max(s, axis=-1, keepdims=True)
                e = jnp.exp(s - m)
                l = jnp.sum(e, axis=-1, keepdims=True)
                pv = jnp.dot(e.astype(BF16), v, preferred_element_type=F32)
                num[bi, os_, :] = pv
                mm[bi, os_, :] = jnp.broadcast_to(m, (BLK, HEAD_DIM))
                ll[bi, os_, :] = jnp.broadcast_to(l, (BLK, HEAD_DIM))
                return 0

            lax.fori_loop(0, CHUNK // BLK, blk_body, 0)

        def comb_body(rb, _):
            r0 = pl.multiple_of(rb * BLK, BLK)
            rows = pl.ds(r0, BLK)
            m0, m1, m2 = mm[0, rows, :], mm[1, rows, :], mm[2, rows, :]
            mx = jnp.maximum(jnp.maximum(m0, m1), m2)
            a0, a1, a2 = jnp.exp(m0 - mx), jnp.exp(m1 - mx), jnp.exp(m2 - mx)
            den = a0 * ll[0, rows, :] + a1 * ll[1, rows, :] + a2 * ll[2, rows, :]
            o = (a0 * num[0, rows, :] + a1 * num[1, rows, :] + a2 * num[2, rows, :]) / den
            o_ref[pl.ds(base + r0, BLK), :] = o.astype(o_ref.dtype)
            return 0

        lax.fori_loop(0, CHUNK // BLK, comb_body, 0)
        return 0

    lax.fori_loop(0, seq // CHUNK, chunk_body, 0)


def _attention(z3):
    B, S, _ = z3.shape
    blk = lambda off: pl.BlockSpec((None, S, HEAD_DIM), lambda b, h, off=off: (b, 0, off + h))
    return pl.pallas_call(
        functools.partial(_attn_kernel, seq=S),
        grid=(B, N_HEADS),
        in_specs=[blk(0), blk(N_HEADS), blk(2 * N_HEADS)],
        out_specs=pl.BlockSpec((None, S, HEAD_DIM), lambda b, h: (b, 0, h)),
        out_shape=jax.ShapeDtypeStruct((B, S, D_ATTN), BF16),
        scratch_shapes=[
            pltpu.VMEM((S, HEAD_DIM), F32),
            pltpu.VMEM((CHUNK + S, HEAD_DIM), F32),
            pltpu.VMEM((CHUNK + S, HEAD_DIM), F32),
            pltpu.VMEM((3, CHUNK, HEAD_DIM), F32),
            pltpu.VMEM((3, CHUNK, HEAD_DIM), F32),
            pltpu.VMEM((3, CHUNK, HEAD_DIM), F32),
            pltpu.VMEM((2, BLK, 2 * BLK), F32),
        ],
        compiler_params=_cparams(("parallel", "parallel")),
        name="dilated_attn",
    )(z3, z3, z3)


SUBLANES = 8
LANES = 128
HALO = 32
CONV_ROWS = 64
NORM_ROWS = 16


def _conv_kernel(cv_ref, cg_ref, hv_ref, hg_ref, w_ref, b_ref, g_ref, bl_ref, o_ref, ubuf, ybuf, *, ts):
    i = pl.program_id(1)
    C = ubuf.shape[1]
    hu = hv_ref[...].astype(F32) * jax.nn.sigmoid(hg_ref[...].astype(F32))
    ubuf[pl.ds(0, HALO), :] = jnp.where(i > 0, hu, 0.0)
    ubuf[pl.ds(HALO, ts), :] = cv_ref[...].astype(F32) * jax.nn.sigmoid(cg_ref[...].astype(F32))

    off = HALO - (CONV_WIDTH - 1)
    win_rows = CONV_ROWS + HALO
    ncol = C // LANES

    def blk(idx, _):
        t0 = pl.multiple_of((idx // ncol) * CONV_ROWS, CONV_ROWS)
        c0 = pl.multiple_of((idx % ncol) * LANES, LANES)
        win = ubuf[pl.ds(t0, win_rows), pl.ds(c0, LANES)]
        acc = jnp.zeros((CONV_ROWS, LANES), F32)
        for s in range(SUBLANES):
            ys = win if s == 0 else pltpu.roll(win, win_rows - s, axis=0)
            for a in range(HALO // SUBLANES + 1):
                j = SUBLANES * a + s - off
                if 0 <= j < CONV_WIDTH:
                    acc = acc + ys[SUBLANES * a:SUBLANES * a + CONV_ROWS] * w_ref[pl.ds(j, 1), pl.ds(c0, LANES)]
        ybuf[pl.ds(t0, CONV_ROWS), pl.ds(c0, LANES)] = acc
        return 0

    lax.fori_loop(0, (ts // CONV_ROWS) * ncol, blk, 0)

    def grp(gi, _):
        t0 = pl.multiple_of(gi * NORM_ROWS, NORM_ROWS)
        y = ybuf[pl.ds(t0, NORM_ROWS), :] + b_ref[...]
        mu = jnp.mean(y, axis=-1, keepdims=True)
        yc = y - mu
        var = jnp.mean(yc * yc, axis=-1, keepdims=True)
        yn = yc * lax.rsqrt(var + EPS) * g_ref[...] + bl_ref[...]
        o_ref[pl.ds(t0, NORM_ROWS), :] = (yn * jax.nn.sigmoid(yn)).astype(o_ref.dtype)
        return 0

    lax.fori_loop(0, ts // NORM_ROWS, grp, 0)


def _conv(z3, w_dw, b_dw, g_ln, b_ln, ts=512):
    B, S, _ = z3.shape
    C = w_dw.shape[1]
    cvb, cgb = 3 * D_ATTN // C, 3 * D_ATTN // C + 1
    hpt = ts // HALO
    halo = lambda col: pl.BlockSpec(
        (None, HALO, C), lambda b, i, col=col: (b, jnp.maximum(i * hpt - 1, 0), col))
    vec = pl.BlockSpec((1, C), lambda b, i: (0, 0))
    return pl.pallas_call(
        functools.partial(_conv_kernel, ts=ts),
        grid=(B, S // ts),
        in_specs=[
            pl.BlockSpec((None, ts, C), lambda b, i: (b, i, cvb)),
            pl.BlockSpec((None, ts, C), lambda b, i: (b, i, cgb)),
            halo(cvb), halo(cgb),
            pl.BlockSpec((CONV_WIDTH, C), lambda b, i: (0, 0)),
            vec, vec, vec,
        ],
        out_specs=pl.BlockSpec((None, ts, C), lambda b, i: (b, i, 0)),
        out_shape=jax.ShapeDtypeStruct((B, S, C), BF16),
        scratch_shapes=[pltpu.VMEM((HALO + ts, C), F32), pltpu.VMEM((ts, C), F32)],
        compiler_params=_cparams(("parallel", "parallel")),
        name="conformer_conv",
    )(z3, z3, z3, z3, w_dw, b_dw, g_ln, b_ln)


def _out_proj_kernel(oa_ref, oc_ref, x_ref, wa_ref, wc_ref, g_ref, h_ref, f_ref):
    acc = jnp.dot(oa_ref[...], wa_ref[...], preferred_element_type=F32)
    acc = acc + jnp.dot(oc_ref[...], wc_ref[...], preferred_element_type=F32)
    h = x_ref[...] + acc
    h_ref[...] = h
    f_ref[...] = _rms(h, g_ref[...]).astype(f_ref.dtype)


def _out_proj(oa, oc, x2, w_out, g, tm=512):
    M, D = x2.shape
    Ka, Kc = oa.shape[1], oc.shape[1]
    row = lambda n: pl.BlockSpec((tm, n), lambda i: (i, 0))
    return pl.pallas_call(
        _out_proj_kernel,
        grid=(M // tm,),
        in_specs=[
            row(Ka), row(Kc), row(D),
            pl.BlockSpec((Ka, D), lambda i: (0, 0)),
            pl.BlockSpec((Kc, D), lambda i: (Ka // Kc, 0)),
            pl.BlockSpec((1, D), lambda i: (0, 0)),
        ],
        out_specs=[row(D), row(D)],
        out_shape=[jax.ShapeDtypeStruct((M, D), F32), jax.ShapeDtypeStruct((M, D), BF16)],
        compiler_params=_cparams(("parallel",)),
        name="out_proj",
    )(oa, oc, x2, w_out, w_out, g)


def _ffn_kernel(f_ref, wg_ref, wu_ref, wd_ref, o_ref):
    @pl.when(pl.program_id(1) == 0)
    def _():
        o_ref[...] = jnp.zeros_like(o_ref)

    f = f_ref[...]
    g = jnp.dot(f, wg_ref[...], preferred_element_type=F32)
    u = jnp.dot(f, wu_ref[...], preferred_element_type=F32)
    act = (g * jax.nn.sigmoid(g) * u).astype(BF16)
    o_ref[...] += jnp.dot(act, wd_ref[...], preferred_element_type=F32)


def _ffn(f, wg, wu, wd, tm=1024, tf=512):
    M, D = f.shape
    F = wg.shape[1]
    return pl.pallas_call(
        _ffn_kernel,
        grid=(M // tm, F // tf),
        in_specs=[
            pl.BlockSpec((tm, D), lambda i, j: (i, 0)),
            pl.BlockSpec((D, tf), lambda i, j: (0, j)),
            pl.BlockSpec((D, tf), lambda i, j: (0, j)),
            pl.BlockSpec((tf, D), lambda i, j: (j, 0)),
        ],
        out_specs=pl.BlockSpec((tm, D), lambda i, j: (i, 0)),
        out_shape=jax.ShapeDtypeStruct((M, D), F32),
        compiler_params=_cparams(("parallel", "arbitrary")),
        name="ffn",
    )(f, wg, wu, wd)


def _ple_kernel(h_ref, d_ref, p_ref, wpg_ref, bpg_ref, wple_ref, gp_ref, gf_ref, o_ref, *, final):
    h2 = h_ref[...] + d_ref[...]
    a = _rms(h2, gp_ref[...]).astype(BF16)
    gte = jax.nn.sigmoid(jnp.dot(a, wpg_ref[...], preferred_element_type=F32) + bpg_ref[...])
    ple = jnp.dot(p_ref[...].astype(BF16), wple_ref[...], preferred_element_type=F32)
    h3 = h2 + ple * gte
    o_ref[...] = _rms(h3, gf_ref[...]) if final else h3


def _ple(h1, d, p2, wpg, bpg, wple, g_ple, g_final, final, tm=512):
    M, D = h1.shape
    P = p2.shape[1]
    row = lambda n: pl.BlockSpec((tm, n), lambda i: (i, 0))
    full = lambda r, c: pl.BlockSpec((r, c), lambda i: (0, 0))
    return pl.pallas_call(
        functools.partial(_ple_kernel, final=final),
        grid=(M // tm,),
        in_specs=[row(D), row(D), row(P), full(D, D), full(1, D), full(P, D), full(1, D), full(1, D)],
        out_specs=row(D),
        out_shape=jax.ShapeDtypeStruct((M, D), F32),
        compiler_params=_cparams(("parallel",)),
        name="ple_final",
    )(h1, d, p2, wpg, bpg, wple, g_ple, g_final)


def kernel(x, p, g_mix, w_in, w_dw, b_dw, g_conv_ln, b_conv_ln, w_out, g_ffn, w_gate, w_up, w_down,
           g_ple, w_pgate, b_pgate, w_ple, g_final):
    B, S, D = x.shape
    depth = w_in.shape[0]
    M = B * S
    h = x.reshape(M, D)
    out = None
    for i in range(depth):
        last = i == depth - 1
        z = _in_proj(h, g_mix[i][None], w_in[i].astype(BF16))
        z3 = z.reshape(B, S, -1)
        oa = _attention(z3).reshape(M, -1)
        oc = _conv(z3, w_dw[i, :, 0, :], b_dw[i][None], g_conv_ln[i][None], b_conv_ln[i][None]).reshape(M, -1)
        h1, f = _out_proj(oa, oc, h, w_out[i].astype(BF16), g_ffn[i][None])
        d = _ffn(f, w_gate[i].astype(BF16), w_up[i].astype(BF16), w_down[i].astype(BF16))
        out = _ple(h1, d, p[i].reshape(M, -1), w_pgate[i].astype(BF16), b_pgate[i][None],
                   w_ple[i].astype(BF16), g_ple[i][None], g_final[None], last)
        h = out
    return out.reshape(B, S, D)
```

```python
import functools

import jax
import jax.numpy as jnp
from jax import lax
from jax.experimental import pallas as pl
from jax.experimental.pallas import tpu as pltpu

F32 = jnp.float32
BF16 = jnp.bfloat16

EPS = 1e-6
N_HEADS = 8
HEAD_DIM = 128
D_ATTN = N_HEADS * HEAD_DIM
BLK = 128
DILATIONS = (1, 4, 16)
CONV_WIDTH = 31
CHUNK = DILATIONS[-1] * BLK
NEG = -1e30

VMEM_LIMIT = 56 * 1024 * 1024


def _cparams(sem):
    return pltpu.CompilerParams(dimension_semantics=sem, vmem_limit_bytes=VMEM_LIMIT)


def _rms(x, g):
    return x * lax.rsqrt(jnp.mean(x * x, axis=-1, keepdims=True) + EPS) * g


def _in_proj_kernel(x_ref, g_ref, w_ref, o_ref, a_ref):
    @pl.when(pl.program_id(1) == 0)
    def _():
        a_ref[...] = _rms(x_ref[...], g_ref[...]).astype(BF16)

    z = jnp.dot(a_ref[...], w_ref[...], preferred_element_type=F32)
    tn = o_ref.shape[1]
    col = pl.program_id(1) * tn + lax.broadcasted_iota(jnp.int32, (1, tn), 1)
    o_ref[...] = (z * jnp.where(col < D_ATTN, HEAD_DIM ** -0.5, 1.0)).astype(o_ref.dtype)


def _in_proj(x2, g, w, tm=1024, tn=1024):
    M, D = x2.shape
    N = w.shape[1]
    return pl.pallas_call(
        _in_proj_kernel,
        grid=(M // tm, N // tn),
        in_specs=[
            pl.BlockSpec((tm, D), lambda i, j: (i, 0)),
            pl.BlockSpec((1, D), lambda i, j: (0, 0)),
            pl.BlockSpec((D, tn), lambda i, j: (0, j)),
        ],
        out_specs=pl.BlockSpec((tm, tn), lambda i, j: (i, j)),
        out_shape=jax.ShapeDtypeStruct((M, N), BF16),
        scratch_shapes=[pltpu.VMEM((tm, D), BF16)],
        compiler_params=_cparams(("parallel", "arbitrary")),
        name="in_proj",
    )(x2, g, w)


NSLAB = 4
SLAB_CHUNK = CHUNK // NSLAB
PIECE = BLK // NSLAB


def _attn_kernel(*refs, seq):
    q_refs, k_refs, v_refs = refs[0:NSLAB], refs[NSLAB:2 * NSLAB], refs[2 * NSLAB:3 * NSLAB]
    o_ref = refs[3 * NSLAB]
    qf, kf, vf, num, mm, ll, onat, bias = refs[3 * NSLAB + 1:]
    rows4 = seq // NSLAB
    cp = 512

    def cast_body(i, _):
        rr = pl.multiple_of(i * cp, cp)
        for r in range(NSLAB):
            qf[r, pl.ds(rr, cp), :] = q_refs[r][pl.ds(rr, cp), :].astype(F32)
            kf[r, pl.ds(rr, cp), :] = k_refs[r][pl.ds(rr, cp), :].astype(F32)
            vf[r, pl.ds(rr, cp), :] = v_refs[r][pl.ds(rr, cp), :].astype(F32)
        return 0

    lax.fori_loop(0, rows4 // cp, cast_body, 0)

    qi = lax.broadcasted_iota(jnp.int32, (BLK, 2 * BLK), 0)
    kj = lax.broadcasted_iota(jnp.int32, (BLK, 2 * BLK), 1)

    def put(slot, dist):
        bias[slot] = jnp.where((dist >= 0) & (dist <= BLK), 0.0, NEG).astype(F32)

    put(0, qi + BLK - kj)
    put(1, qi - kj)
    d1 = NSLAB * (qi % PIECE - kj % (2 * PIECE)) + (qi // PIECE - kj // (2 * PIECE))
    put(2, d1 + BLK)
    put(3, d1)

    ones = jnp.ones((2 * BLK, HEAD_DIM), BF16)

    def block(q, k, v, bias_slot):
        s = lax.dot_general(q, k, (((1,), (1,)), ((), ())), preferred_element_type=F32)
        s = s + bias[bias_slot]
        m = jnp.max(s, axis=-1, keepdims=True)
        e = jnp.exp(s - m).astype(BF16)
        pv = jnp.dot(e, jnp.concatenate([v, ones], axis=1), preferred_element_type=F32)
        return pv[:, :HEAD_DIM], jnp.broadcast_to(m, (BLK, HEAD_DIM)), pv[:, HEAD_DIM:]

    def chunk_body(c, _):
        r0 = pl.multiple_of(c * SLAB_CHUNK, SLAB_CHUNK)
        first_chunk = (c == 0).astype(jnp.int32)

        def d16_body(mres, _):
            k0 = jnp.maximum(r0 - SLAB_CHUNK, 0) + mres
            for r in range(NSLAB):
                q = qf[r, pl.ds(r0 + mres, BLK, stride=NSLAB), :].astype(BF16)
                k = kf[r, pl.ds(k0, 2 * BLK, stride=NSLAB), :].astype(BF16)
                v = vf[r, pl.ds(k0, 2 * BLK, stride=NSLAB), :].astype(BF16)
                pv, m, l = block(q, k, v, first_chunk)
                dst = pl.ds(mres, BLK, stride=NSLAB)
                num[r, dst, :] = pv
                mm[r, dst, :] = m
                ll[r, dst, :] = l
            return 0

        lax.fori_loop(0, NSLAB, d16_body, 0, unroll=4)

        def d4_body(nb, _):
            loc = pl.multiple_of(nb * BLK, BLK)
            q0 = r0 + loc
            first = (q0 == 0).astype(jnp.int32)
            k0 = pl.multiple_of(jnp.maximum(q0 - BLK, 0), BLK)
            for r in range(NSLAB):
                q = q_refs[r][pl.ds(q0, BLK), :]
                k = k_refs[r][pl.ds(k0, 2 * BLK), :]
                v = v_refs[r][pl.ds(k0, 2 * BLK), :]
                pv, m, l = block(q, k, v, first)
                dst = pl.ds(loc, BLK)
                num[NSLAB + r, dst, :] = pv
                mm[NSLAB + r, dst, :] = m
                ll[NSLAB + r, dst, :] = l
            return 0

        lax.fori_loop(0, SLAB_CHUNK // BLK, d4_body, 0, unroll=4)

        def d1_body(nb, _):
            loc = pl.multiple_of(nb * PIECE, PIECE)
            q0 = r0 + loc
            first = (q0 == 0).astype(jnp.int32)
            k0 = pl.multiple_of(jnp.maximum(q0 - PIECE, 0), PIECE)
            q = jnp.concatenate([q_refs[r][pl.ds(q0, PIECE), :] for r in range(NSLAB)], axis=0)
            k = jnp.concatenate([k_refs[r][pl.ds(k0, 2 * PIECE), :] for r in range(NSLAB)], axis=0)
            v = jnp.concatenate([v_refs[r][pl.ds(k0, 2 * PIECE), :] for r in range(NSLAB)], axis=0)
            pv, m, l = block(q, k, v, 2 + first)
            for r in range(NSLAB):
                dst = pl.ds(loc, PIECE)
                src = slice(r * PIECE, (r + 1) * PIECE)
                num[2 * NSLAB + r, dst, :] = pv[src]
                mm[2 * NSLAB + r, dst, :] = m[src]
                ll[2 * NSLAB + r, dst, :] = l[src]
            return 0

        lax.fori_loop(0, SLAB_CHUNK // PIECE, d1_body, 0, unroll=16)

        def comb_body(g, _):
            loc = pl.multiple_of(g * BLK, BLK)
            rows = pl.ds(loc, BLK)
            for r in range(NSLAB):
                m0, m1, m2 = mm[r, rows, :], mm[NSLAB + r, rows, :], mm[2 * NSLAB + r, rows, :]
                mx = jnp.maximum(jnp.maximum(m0, m1), m2)
                a0, a1, a2 = jnp.exp(m0 - mx), jnp.exp(m1 - mx), jnp.exp(m2 - mx)
                den = a0 * ll[r, rows, :] + a1 * ll[NSLAB + r, rows, :] + a2 * ll[2 * NSLAB + r, rows, :]
                o = (a0 * num[r, rows, :] + a1 * num[NSLAB + r, rows, :]
                     + a2 * num[2 * NSLAB + r, rows, :]) / den
                onat[pl.ds(r, BLK, stride=NSLAB), :] = o
            t0 = pl.multiple_of(c * CHUNK + g * (NSLAB * BLK), NSLAB * BLK)
            o_ref[pl.ds(t0, NSLAB * BLK), :] = onat[...].astype(o_ref.dtype)
            return 0

        lax.fori_loop(0, SLAB_CHUNK // BLK, comb_body, 0)
        return 0

    lax.fori_loop(0, seq // CHUNK, chunk_body, 0)


def _attention(z3):
    B, S, N = z3.shape
    rows4 = S // NSLAB
    z4 = z3.reshape(B, rows4, NSLAB * N)
    cols = N // HEAD_DIM

    def slab(r, off):
        return pl.BlockSpec((None, rows4, HEAD_DIM), lambda b, h, r=r, off=off: (b, 0, r * cols + off + h))

    in_specs = [slab(r, off) for off in (0, N_HEADS, 2 * N_HEADS) for r in range(NSLAB)]
    return pl.pallas_call(
        functools.partial(_attn_kernel, seq=S),
        grid=(B, N_HEADS),
        in_specs=in_specs,
        out_specs=pl.BlockSpec((None, S, HEAD_DIM), lambda b, h: (b, 0, h)),
        out_shape=jax.ShapeDtypeStruct((B, S, D_ATTN), BF16),
        scratch_shapes=[
            pltpu.VMEM((NSLAB, rows4, HEAD_DIM), F32),
            pltpu.VMEM((NSLAB, rows4, HEAD_DIM), F32),
            pltpu.VMEM((NSLAB, rows4, HEAD_DIM), F32),
            pltpu.VMEM((3 * NSLAB, SLAB_CHUNK, HEAD_DIM), F32),
            pltpu.VMEM((3 * NSLAB, SLAB_CHUNK, HEAD_DIM), F32),
            pltpu.VMEM((3 * NSLAB, SLAB_CHUNK, HEAD_DIM), F32),
            pltpu.VMEM((NSLAB * BLK, HEAD_DIM), F32),
            pltpu.VMEM((4, BLK, 2 * BLK), F32),
        ],
        compiler_params=_cparams(("parallel", "parallel")),
        name="dilated_attn",
    )(*([z4] * (3 * NSLAB)))


SUBLANES = 8
LANES = 128
HALO = 32
CONV_ROWS = 64
NORM_ROWS = 16


def _conv_kernel(cv_ref, cg_ref, hv_ref, hg_ref, w_ref, b_ref, g_ref, bl_ref, o_ref, ubuf, ybuf, *, ts):
    i = pl.program_id(1)
    C = ubuf.shape[1]
    hu = hv_ref[...].astype(F32) * jax.nn.sigmoid(hg_ref[...].astype(F32))
    ubuf[pl.ds(0, HALO), :] = jnp.where(i > 0, hu, 0.0)
    ubuf[pl.ds(HALO, ts), :] = cv_ref[...].astype(F32) * jax.nn.sigmoid(cg_ref[...].astype(F32))

    off = HALO - (CONV_WIDTH - 1)
    win_rows = CONV_ROWS + HALO
    ncol = C // LANES

    def blk(idx, _):
        t0 = pl.multiple_of((idx // ncol) * CONV_ROWS, CONV_ROWS)
        c0 = pl.multiple_of((idx % ncol) * LANES, LANES)
        win = ubuf[pl.ds(t0, win_rows), pl.ds(c0, LANES)]
        acc = jnp.zeros((CONV_ROWS, LANES), F32)
        for s in range(SUBLANES):
            ys = win if s == 0 else pltpu.roll(win, win_rows - s, axis=0)
            for a in range(HALO // SUBLANES + 1):
                j = SUBLANES * a + s - off
                if 0 <= j < CONV_WIDTH:
                    acc = acc + ys[SUBLANES * a:SUBLANES * a + CONV_ROWS] * w_ref[pl.ds(j, 1), pl.ds(c0, LANES)]
        ybuf[pl.ds(t0, CONV_ROWS), pl.ds(c0, LANES)] = acc
        return 0

    lax.fori_loop(0, (ts // CONV_ROWS) * ncol, blk, 0)

    def grp(gi, _):
        t0 = pl.multiple_of(gi * NORM_ROWS, NORM_ROWS)
        y = ybuf[pl.ds(t0, NORM_ROWS), :] + b_ref[...]
        mu = jnp.mean(y, axis=-1, keepdims=True)
        yc = y - mu
        var = jnp.mean(yc * yc, axis=-1, keepdims=True)
        yn = yc * lax.rsqrt(var + EPS) * g_ref[...] + bl_ref[...]
        o_ref[pl.ds(t0, NORM_ROWS), :] = (yn * jax.nn.sigmoid(yn)).astype(o_ref.dtype)
        return 0

    lax.fori_loop(0, ts // NORM_ROWS, grp, 0, unroll=4)


def _conv(z3, w_dw, b_dw, g_ln, b_ln, ts=512):
    B, S, _ = z3.shape
    C = w_dw.shape[1]
    cvb, cgb = 3 * D_ATTN // C, 3 * D_ATTN // C + 1
    hpt = ts // HALO
    halo = lambda col: pl.BlockSpec(
        (None, HALO, C), lambda b, i, col=col: (b, jnp.maximum(i * hpt - 1, 0), col))
    vec = pl.BlockSpec((1, C), lambda b, i: (0, 0))
    return pl.pallas_call(
        functools.partial(_conv_kernel, ts=ts),
        grid=(B, S // ts),
        in_specs=[
            pl.BlockSpec((None, ts, C), lambda b, i: (b, i, cvb)),
            pl.BlockSpec((None, ts, C), lambda b, i: (b, i, cgb)),
            halo(cvb), halo(cgb),
            pl.BlockSpec((CONV_WIDTH, C), lambda b, i: (0, 0)),
            vec, vec, vec,
        ],
        out_specs=pl.BlockSpec((None, ts, C), lambda b, i: (b, i, 0)),
        out_shape=jax.ShapeDtypeStruct((B, S, C), BF16),
        scratch_shapes=[pltpu.VMEM((HALO + ts, C), F32), pltpu.VMEM((ts, C), F32)],
        compiler_params=_cparams(("parallel", "parallel")),
        name="conformer_conv",
    )(z3, z3, z3, z3, w_dw, b_dw, g_ln, b_ln)


def _out_proj_kernel(oa_ref, oc_ref, x_ref, wa_ref, wc_ref, g_ref, h_ref, f_ref):
    acc = jnp.dot(oa_ref[...], wa_ref[...], preferred_element_type=F32)
    acc = acc + jnp.dot(oc_ref[...], wc_ref[...], preferred_element_type=F32)
    h = x_ref[...] + acc
    h_ref[...] = h
    f_ref[...] = _rms(h, g_ref[...]).astype(f_ref.dtype)


def _out_proj(oa, oc, x2, w_out, g, tm=512):
    M, D = x2.shape
    Ka, Kc = oa.shape[1], oc.shape[1]
    row = lambda n: pl.BlockSpec((tm, n), lambda i: (i, 0))
    return pl.pallas_call(
        _out_proj_kernel,
        grid=(M // tm,),
        in_specs=[
            row(Ka), row(Kc), row(D),
            pl.BlockSpec((Ka, D), lambda i: (0, 0)),
            pl.BlockSpec((Kc, D), lambda i: (Ka // Kc, 0)),
            pl.BlockSpec((1, D), lambda i: (0, 0)),
        ],
        out_specs=[row(D), row(D)],
        out_shape=[jax.ShapeDtypeStruct((M, D), F32), jax.ShapeDtypeStruct((M, D), BF16)],
        compiler_params=_cparams(("parallel",)),
        name="out_proj",
    )(oa, oc, x2, w_out, w_out, g)


def _ffn_kernel(f_ref, wg_ref, wu_ref, wd_ref, o_ref):
    @pl.when(pl.program_id(1) == 0)
    def _():
        o_ref[...] = jnp.zeros_like(o_ref)

    f = f_ref[...]
    g = jnp.dot(f, wg_ref[...], preferred_element_type=F32)
    u = jnp.dot(f, wu_ref[...], preferred_element_type=F32)
    act = (g * jax.nn.sigmoid(g) * u).astype(BF16)
    o_ref[...] += jnp.dot(act, wd_ref[...], preferred_element_type=F32)


def _ffn(f, wg, wu, wd, tm=1024, tf=512):
    M, D = f.shape
    F = wg.shape[1]
    return pl.pallas_call(
        _ffn_kernel,
        grid=(M // tm, F // tf),
        in_specs=[
            pl.BlockSpec((tm, D), lambda i, j: (i, 0)),
            pl.BlockSpec((D, tf), lambda i, j: (0, j)),
            pl.BlockSpec((D, tf), lambda i, j: (0, j)),
            pl.BlockSpec((tf, D), lambda i, j: (j, 0)),
        ],
        out_specs=pl.BlockSpec((tm, D), lambda i, j: (i, 0)),
        out_shape=jax.ShapeDtypeStruct((M, D), F32),
        compiler_params=_cparams(("parallel", "arbitrary")),
        name="ffn",
    )(f, wg, wu, wd)


def _ple_kernel(h_ref, d_ref, p_ref, wpg_ref, bpg_ref, wple_ref, gp_ref, gf_ref, o_ref, *, final):
    h2 = h_ref[...] + d_ref[...]
    a = _rms(h2, gp_ref[...]).astype(BF16)
    gte = jax.nn.sigmoid(jnp.dot(a, wpg_ref[...], preferred_element_type=F32) + bpg_ref[...])
    ple = jnp.dot(p_ref[...].astype(BF16), wple_ref[...], preferred_element_type=F32)
    h3 = h2 + ple * gte
    o_ref[...] = _rms(h3, gf_ref[...]) if final else h3


def _ple(h1, d, p2, wpg, bpg, wple, g_ple, g_final, final, tm=512):
    M, D = h1.shape
    P = p2.shape[1]
    row = lambda n: pl.BlockSpec((tm, n), lambda i: (i, 0))
    full = lambda r, c: pl.BlockSpec((r, c), lambda i: (0, 0))
    return pl.pallas_call(
        functools.partial(_ple_kernel, final=final),
        grid=(M // tm,),
        in_specs=[row(D), row(D), row(P), full(D, D), full(1, D), full(P, D), full(1, D), full(1, D)],
        out_specs=row(D),
        out_shape=jax.ShapeDtypeStruct((M, D), F32),
        compiler_params=_cparams(("parallel",)),
        name="ple_final",
    )(h1, d, p2, wpg, bpg, wple, g_ple, g_final)


def kernel(x, p, g_mix, w_in, w_dw, b_dw, g_conv_ln, b_conv_ln, w_out, g_ffn, w_gate, w_up, w_down,
           g_ple, w_pgate, b_pgate, w_ple, g_final):
    B, S, D = x.shape
    depth = w_in.shape[0]
    M = B * S
    h = x.reshape(M, D)
    out = None
    for i in range(depth):
        last = i == depth - 1
        z = _in_proj(h, g_mix[i][None], w_in[i].astype(BF16))
        z3 = z.reshape(B, S, -1)
        oa = _attention(z3).reshape(M, -1)
        oc = _conv(z3, w_dw[i, :, 0, :], b_dw[i][None], g_conv_ln[i][None], b_conv_ln[i][None]).reshape(M, -1)
        h1, f = _out_proj(oa, oc, h, w_out[i].astype(BF16), g_ffn[i][None])
        d = _ffn(f, w_gate[i].astype(BF16), w_up[i].astype(BF16), w_down[i].astype(BF16))
        out = _ple(h1, d, p[i].reshape(M, -1), w_pgate[i].astype(BF16), b_pgate[i][None],
                   w_ple[i].astype(BF16), g_ple[i][None], g_final[None], last)
        h = out
    return out.reshape(B, S, D)
```

```python
import functools

import jax
import jax.numpy as jnp
from jax import lax
from jax.experimental import pallas as pl
from jax.experimental.pallas import tpu as pltpu

F32 = jnp.float32
BF16 = jnp.bfloat16

EPS = 1e-6
N_HEADS = 8
HEAD_DIM = 128
D_ATTN = N_HEADS * HEAD_DIM
BLK = 128
DILATIONS = (1, 4, 16)
CONV_WIDTH = 31
CHUNK = DILATIONS[-1] * BLK
NEG = -1e30

VMEM_LIMIT = 56 * 1024 * 1024


def _cparams(sem):
    return pltpu.CompilerParams(dimension_semantics=sem, vmem_limit_bytes=VMEM_LIMIT)


def _rms(x, g):
    return x * lax.rsqrt(jnp.mean(x * x, axis=-1, keepdims=True) + EPS) * g


def _in_proj_kernel(x_ref, g_ref, w_ref, o_ref, a_ref):
    @pl.when(pl.program_id(1) == 0)
    def _():
        a_ref[...] = _rms(x_ref[...], g_ref[...]).astype(BF16)

    z = jnp.dot(a_ref[...], w_ref[...], preferred_element_type=F32)
    tn = o_ref.shape[1]
    col = pl.program_id(1) * tn + lax.broadcasted_iota(jnp.int32, (1, tn), 1)
    o_ref[...] = (z * jnp.where(col < D_ATTN, HEAD_DIM ** -0.5, 1.0)).astype(o_ref.dtype)


def _in_proj(x2, g, w, tm=1024, tn=1024):
    M, D = x2.shape
    N = w.shape[1]
    return pl.pallas_call(
        _in_proj_kernel,
        grid=(M // tm, N // tn),
        in_specs=[
            pl.BlockSpec((tm, D), lambda i, j: (i, 0)),
            pl.BlockSpec((1, D), lambda i, j: (0, 0)),
            pl.BlockSpec((D, tn), lambda i, j: (0, j)),
        ],
        out_specs=pl.BlockSpec((tm, tn), lambda i, j: (i, j)),
        out_shape=jax.ShapeDtypeStruct((M, N), BF16),
        scratch_shapes=[pltpu.VMEM((tm, D), BF16)],
        compiler_params=_cparams(("parallel", "arbitrary")),
        name="in_proj",
    )(x2, g, w)


NSLAB = 4
SLAB_CHUNK = CHUNK // NSLAB
PIECE = BLK // NSLAB


def _attn_kernel(q_ref, k_ref, v_ref, o_ref, qf, kf, vf, num, mm, ll, stage, onat, bias, *, seq):
    cp = NSLAB * BLK

    def cast_body(i, _):
        t0 = pl.multiple_of(i * cp, cp)
        s0 = pl.multiple_of(i * BLK, BLK)
        for j, (src, dst) in enumerate(((q_ref, qf), (k_ref, kf), (v_ref, vf))):
            stage[j] = src[pl.ds(t0, cp), :].astype(F32)
            for r in range(NSLAB):
                dst[r, pl.ds(s0, BLK), :] = stage[j, pl.ds(r, BLK, stride=NSLAB), :]
        return 0

    lax.fori_loop(0, seq // cp, cast_body, 0)

    qi = lax.broadcasted_iota(jnp.int32, (BLK, 2 * BLK), 0)
    kj = lax.broadcasted_iota(jnp.int32, (BLK, 2 * BLK), 1)

    def put(slot, dist):
        bias[slot] = jnp.where((dist >= 0) & (dist <= BLK), 0.0, NEG).astype(F32)

    put(0, qi + BLK - kj)
    put(1, qi - kj)
    d1 = NSLAB * (qi % PIECE - kj % (2 * PIECE)) + (qi // PIECE - kj // (2 * PIECE))
    put(2, d1 + BLK)
    put(3, d1)

    ones = jnp.ones((2 * BLK, HEAD_DIM), BF16)

    def block(q, k, v, bias_slot):
        s = lax.dot_general(q, k, (((1,), (1,)), ((), ())), preferred_element_type=F32)
        s = s + bias[bias_slot]
        m = jnp.max(s, axis=-1, keepdims=True)
        e = jnp.exp(s - m).astype(BF16)
        pv = jnp.dot(e, jnp.concatenate([v, ones], axis=1), preferred_element_type=F32)
        return pv[:, :HEAD_DIM], jnp.broadcast_to(m, (BLK, HEAD_DIM)), pv[:, HEAD_DIM:]

    def chunk_body(c, _):
        r0 = pl.multiple_of(c * SLAB_CHUNK, SLAB_CHUNK)
        first_chunk = jnp.where(c == 0, 1, 0)

        def d16_body(mres, _):
            k0 = jnp.maximum(r0 - SLAB_CHUNK, 0) + mres
            for r in range(NSLAB):
                q = qf[r, pl.ds(r0 + mres, BLK, stride=NSLAB), :].astype(BF16)
                k = kf[r, pl.ds(k0, 2 * BLK, stride=NSLAB), :].astype(BF16)
                v = vf[r, pl.ds(k0, 2 * BLK, stride=NSLAB), :].astype(BF16)
                pv, m, l = block(q, k, v, first_chunk)
                dst = pl.ds(mres, BLK, stride=NSLAB)
                num[r, dst, :] = pv
                mm[r, dst, :] = m
                ll[r, dst, :] = l
            return 0

        lax.fori_loop(0, NSLAB, d16_body, 0, unroll=4)

        def d4_body(nb, _):
            loc = pl.multiple_of(nb * BLK, BLK)
            q0 = r0 + loc
            first = jnp.where(q0 == 0, 1, 0)
            k0 = pl.multiple_of(jnp.maximum(q0 - BLK, 0), BLK)
            for r in range(NSLAB):
                q = qf[r, pl.ds(q0, BLK), :].astype(BF16)
                k = kf[r, pl.ds(k0, 2 * BLK), :].astype(BF16)
                v = vf[r, pl.ds(k0, 2 * BLK), :].astype(BF16)
                pv, m, l = block(q, k, v, first)
                dst = pl.ds(loc, BLK)
                num[NSLAB + r, dst, :] = pv
                mm[NSLAB + r, dst, :] = m
                ll[NSLAB + r, dst, :] = l
            return 0

        lax.fori_loop(0, SLAB_CHUNK // BLK, d4_body, 0, unroll=4)

        def d1_body(nb, _):
            loc = pl.multiple_of(nb * PIECE, PIECE)
            q0 = r0 + loc
            first = jnp.where(q0 == 0, 1, 0)
            k0 = pl.multiple_of(jnp.maximum(q0 - PIECE, 0), PIECE)
            gather = lambda ref, start, n: jnp.concatenate(
                [ref[r, pl.ds(start, n), :] for r in range(NSLAB)], axis=0).astype(BF16)
            q = gather(qf, q0, PIECE)
            k = gather(kf, k0, 2 * PIECE)
            v = gather(vf, k0, 2 * PIECE)
            pv, m, l = block(q, k, v, 2 + first)
            for r in range(NSLAB):
                dst = pl.ds(loc, PIECE)
                src = slice(r * PIECE, (r + 1) * PIECE)
                num[2 * NSLAB + r, dst, :] = pv[src]
                mm[2 * NSLAB + r, dst, :] = m[src]
                ll[2 * NSLAB + r, dst, :] = l[src]
            return 0

        lax.fori_loop(0, SLAB_CHUNK // PIECE, d1_body, 0, unroll=16)

        def comb_body(g, _):
            loc = pl.multiple_of(g * BLK, BLK)
            rows = pl.ds(loc, BLK)
            for r in range(NSLAB):
                m0, m1, m2 = mm[r, rows, :], mm[NSLAB + r, rows, :], mm[2 * NSLAB + r, rows, :]
                mx = jnp.maximum(jnp.maximum(m0, m1), m2)
                a0, a1, a2 = jnp.exp(m0 - mx), jnp.exp(m1 - mx), jnp.exp(m2 - mx)
                den = a0 * ll[r, rows, :] + a1 * ll[NSLAB + r, rows, :] + a2 * ll[2 * NSLAB + r, rows, :]
                o = (a0 * num[r, rows, :] + a1 * num[NSLAB + r, rows, :]
                     + a2 * num[2 * NSLAB + r, rows, :]) / den
                onat[pl.ds(r, BLK, stride=NSLAB), :] = o
            t0 = pl.multiple_of(c * CHUNK + g * (NSLAB * BLK), NSLAB * BLK)
            o_ref[pl.ds(t0, NSLAB * BLK), :] = onat[...].astype(o_ref.dtype)
            return 0

        lax.fori_loop(0, SLAB_CHUNK // BLK, comb_body, 0)
        return 0

    lax.fori_loop(0, seq // CHUNK, chunk_body, 0)


def _attention(z3):
    B, S, _ = z3.shape
    rows4 = S // NSLAB
    blk = lambda off: pl.BlockSpec((None, S, HEAD_DIM), lambda b, h, off=off: (b, 0, off + h))
    return pl.pallas_call(
        functools.partial(_attn_kernel, seq=S),
        grid=(B, N_HEADS),
        in_specs=[blk(0), blk(N_HEADS), blk(2 * N_HEADS)],
        out_specs=pl.BlockSpec((None, S, HEAD_DIM), lambda b, h: (b, 0, h)),
        out_shape=jax.ShapeDtypeStruct((B, S, D_ATTN), BF16),
        scratch_shapes=[
            pltpu.VMEM((NSLAB, rows4, HEAD_DIM), F32),
            pltpu.VMEM((NSLAB, rows4, HEAD_DIM), F32),
            pltpu.VMEM((NSLAB, rows4, HEAD_DIM), F32),
            pltpu.VMEM((3 * NSLAB, SLAB_CHUNK, HEAD_DIM), F32),
            pltpu.VMEM((3 * NSLAB, SLAB_CHUNK, HEAD_DIM), F32),
            pltpu.VMEM((3 * NSLAB, SLAB_CHUNK, HEAD_DIM), F32),
            pltpu.VMEM((3, NSLAB * BLK, HEAD_DIM), F32),
            pltpu.VMEM((NSLAB * BLK, HEAD_DIM), F32),
            pltpu.VMEM((4, BLK, 2 * BLK), F32),
        ],
        compiler_params=_cparams(("parallel", "parallel")),
        name="dilated_attn",
    )(z3, z3, z3)


SUBLANES = 8
LANES = 128
HALO = 32
CONV_ROWS = 64
NORM_ROWS = 16


def _conv_kernel(cv_ref, cg_ref, hv_ref, hg_ref, w_ref, b_ref, g_ref, bl_ref, o_ref, ubuf, ybuf, *, ts):
    i = pl.program_id(1)
    C = ubuf.shape[1]
    hu = hv_ref[...].astype(F32) * jax.nn.sigmoid(hg_ref[...].astype(F32))
    ubuf[pl.ds(0, HALO), :] = jnp.where(i > 0, hu, 0.0)
    ubuf[pl.ds(HALO, ts), :] = cv_ref[...].astype(F32) * jax.nn.sigmoid(cg_ref[...].astype(F32))

    off = HALO - (CONV_WIDTH - 1)
    win_rows = CONV_ROWS + HALO
    ncol = C // LANES

    def blk(idx, _):
        t0 = pl.multiple_of((idx // ncol) * CONV_ROWS, CONV_ROWS)
        c0 = pl.multiple_of((idx % ncol) * LANES, LANES)
        win = ubuf[pl.ds(t0, win_rows), pl.ds(c0, LANES)]
        acc = jnp.zeros((CONV_ROWS, LANES), F32)
        for s in range(SUBLANES):
            ys = win if s == 0 else pltpu.roll(win, win_rows - s, axis=0)
            for a in range(HALO // SUBLANES + 1):
                j = SUBLANES * a + s - off
                if 0 <= j < CONV_WIDTH:
                    acc = acc + ys[SUBLANES * a:SUBLANES * a + CONV_ROWS] * w_ref[pl.ds(j, 1), pl.ds(c0, LANES)]
        ybuf[pl.ds(t0, CONV_ROWS), pl.ds(c0, LANES)] = acc
        return 0

    lax.fori_loop(0, (ts // CONV_ROWS) * ncol, blk, 0)

    def grp(gi, _):
        t0 = pl.multiple_of(gi * NORM_ROWS, NORM_ROWS)
        y = ybuf[pl.ds(t0, NORM_ROWS), :] + b_ref[...]
        mu = jnp.mean(y, axis=-1, keepdims=True)
        yc = y - mu
        var = jnp.mean(yc * yc, axis=-1, keepdims=True)
        yn = yc * lax.rsqrt(var + EPS) * g_ref[...] + bl_ref[...]
        o_ref[pl.ds(t0, NORM_ROWS), :] = (yn * jax.nn.sigmoid(yn)).astype(o_ref.dtype)
        return 0

    lax.fori_loop(0, ts // NORM_ROWS, grp, 0, unroll=4)


def _conv(z3, w_dw, b_dw, g_ln, b_ln, ts=512):
    B, S, _ = z3.shape
    C = w_dw.shape[1]
    cvb, cgb = 3 * D_ATTN // C, 3 * D_ATTN // C + 1
    hpt = ts // HALO
    halo = lambda col: pl.BlockSpec(
        (None, HALO, C), lambda b, i, col=col: (b, jnp.maximum(i * hpt - 1, 0), col))
    vec = pl.BlockSpec((1, C), lambda b, i: (0, 0))
    return pl.pallas_call(
        functools.partial(_conv_kernel, ts=ts),
        grid=(B, S // ts),
        in_specs=[
            pl.BlockSpec((None, ts, C), lambda b, i: (b, i, cvb)),
            pl.BlockSpec((None, ts, C), lambda b, i: (b, i, cgb)),
            halo(cvb), halo(cgb),
            pl.BlockSpec((CONV_WIDTH, C), lambda b, i: (0, 0)),
            vec, vec, vec,
        ],
        out_specs=pl.BlockSpec((None, ts, C), lambda b, i: (b, i, 0)),
        out_shape=jax.ShapeDtypeStruct((B, S, C), BF16),
        scratch_shapes=[pltpu.VMEM((HALO + ts, C), F32), pltpu.VMEM((ts, C), F32)],
        compiler_params=_cparams(("parallel", "parallel")),
        name="conformer_conv",
    )(z3, z3, z3, z3, w_dw, b_dw, g_ln, b_ln)


def _out_proj_kernel(oa_ref, oc_ref, x_ref, wa_ref, wc_ref, g_ref, h_ref, f_ref):
    acc = jnp.dot(oa_ref[...], wa_ref[...], preferred_element_type=F32)
    acc = acc + jnp.dot(oc_ref[...], wc_ref[...], preferred_element_type=F32)
    h = x_ref[...] + acc
    h_ref[...] = h
    f_ref[...] = _rms(h, g_ref[...]).astype(f_ref.dtype)


def _out_proj(oa, oc, x2, w_out, g, tm=512):
    M, D = x2.shape
    Ka, Kc = oa.shape[1], oc.shape[1]
    row = lambda n: pl.BlockSpec((tm, n), lambda i: (i, 0))
    return pl.pallas_call(
        _out_proj_kernel,
        grid=(M // tm,),
        in_specs=[
            row(Ka), row(Kc), row(D),
            pl.BlockSpec((Ka, D), lambda i: (0, 0)),
            pl.BlockSpec((Kc, D), lambda i: (Ka // Kc, 0)),
            pl.BlockSpec((1, D), lambda i: (0, 0)),
        ],
        out_specs=[row(D), row(D)],
        out_shape=[jax.ShapeDtypeStruct((M, D), F32), jax.ShapeDtypeStruct((M, D), BF16)],
        compiler_params=_cparams(("parallel",)),
        name="out_proj",
    )(oa, oc, x2, w_out, w_out, g)


def _ffn_kernel(f_ref, wg_ref, wu_ref, wd_ref, o_ref):
    @pl.when(pl.program_id(1) == 0)
    def _():
        o_ref[...] = jnp.zeros_like(o_ref)

    f = f_ref[...]
    g = jnp.dot(f, wg_ref[...], preferred_element_type=F32)
    u = jnp.dot(f, wu_ref[...], preferred_element_type=F32)
    act = (g * jax.nn.sigmoid(g) * u).astype(BF16)
    o_ref[...] += jnp.dot(act, wd_ref[...], preferred_element_type=F32)


def _ffn(f, wg, wu, wd, tm=1024, tf=512):
    M, D = f.shape
    F = wg.shape[1]
    return pl.pallas_call(
        _ffn_kernel,
        grid=(M // tm, F // tf),
        in_specs=[
            pl.BlockSpec((tm, D), lambda i, j: (i, 0)),
            pl.BlockSpec((D, tf), lambda i, j: (0, j)),
            pl.BlockSpec((D, tf), lambda i, j: (0, j)),
            pl.BlockSpec((tf, D), lambda i, j: (j, 0)),
        ],
        out_specs=pl.BlockSpec((tm, D), lambda i, j: (i, 0)),
        out_shape=jax.ShapeDtypeStruct((M, D), F32),
        compiler_params=_cparams(("parallel", "arbitrary")),
        name="ffn",
    )(f, wg, wu, wd)


def _ple_kernel(h_ref, d_ref, p_ref, wpg_ref, bpg_ref, wple_ref, gp_ref, gf_ref, o_ref, *, final):
    h2 = h_ref[...] + d_ref[...]
    a = _rms(h2, gp_ref[...]).astype(BF16)
    gte = jax.nn.sigmoid(jnp.dot(a, wpg_ref[...], preferred_element_type=F32) + bpg_ref[...])
    ple = jnp.dot(p_ref[...].astype(BF16), wple_ref[...], preferred_element_type=F32)
    h3 = h2 + ple * gte
    o_ref[...] = _rms(h3, gf_ref[...]) if final else h3


def _ple(h1, d, p2, wpg, bpg, wple, g_ple, g_final, final, tm=512):
    M, D = h1.shape
    P = p2.shape[1]
    row = lambda n: pl.BlockSpec((tm, n), lambda i: (i, 0))
    full = lambda r, c: pl.BlockSpec((r, c), lambda i: (0, 0))
    return pl.pallas_call(
        functools.partial(_ple_kernel, final=final),
        grid=(M // tm,),
        in_specs=[row(D), row(D), row(P), full(D, D), full(1, D), full(P, D), full(1, D), full(1, D)],
        out_specs=row(D),
        out_shape=jax.ShapeDtypeStruct((M, D), F32),
        compiler_params=_cparams(("parallel",)),
        name="ple_final",
    )(h1, d, p2, wpg, bpg, wple, g_ple, g_final)


def kernel(x, p, g_mix, w_in, w_dw, b_dw, g_conv_ln, b_conv_ln, w_out, g_ffn, w_gate, w_up, w_down,
           g_ple, w_pgate, b_pgate, w_ple, g_final):
    B, S, D = x.shape
    depth = w_in.shape[0]
    M = B * S
    h = x.reshape(M, D)
    out = None
    for i in range(depth):
        last = i == depth - 1
        z = _in_proj(h, g_mix[i][None], w_in[i].astype(BF16))
        z3 = z.reshape(B, S, -1)
        oa = _attention(z3).reshape(M, -1)
        oc = _conv(z3, w_dw[i, :, 0, :], b_dw[i][None], g_conv_ln[i][None], b_conv_ln[i][None]).reshape(M, -1)
        h1, f = _out_proj(oa, oc, h, w_out[i].astype(BF16), g_ffn[i][None])
        d = _ffn(f, w_gate[i].astype(BF16), w_up[i].astype(BF16), w_down[i].astype(BF16))
        out = _ple(h1, d, p[i].reshape(M, -1), w_pgate[i].astype(BF16), b_pgate[i][None],
                   w_ple[i].astype(BF16), g_ple[i][None], g_final[None], last)
        h = out
    return out.reshape(B, S, D)
```

```python
import functools

import jax
import jax.numpy as jnp
from jax import lax
from jax.experimental import pallas as pl
from jax.experimental.pallas import tpu as pltpu

F32 = jnp.float32
BF16 = jnp.bfloat16

EPS = 1e-6
N_HEADS = 8
HEAD_DIM = 128
D_ATTN = N_HEADS * HEAD_DIM
BLK = 128
DILATIONS = (1, 4, 16)
CONV_WIDTH = 31
CHUNK = DILATIONS[-1] * BLK
NEG = -1e30

VMEM_LIMIT = 56 * 1024 * 1024


def _cparams(sem):
    return pltpu.CompilerParams(dimension_semantics=sem, vmem_limit_bytes=VMEM_LIMIT)


def _rms(x, g):
    return x * lax.rsqrt(jnp.mean(x * x, axis=-1, keepdims=True) + EPS) * g

NSLAB = 4
SLAB_CHUNK = CHUNK // NSLAB
PIECE = BLK // NSLAB


def _attn_kernel(q_ref, k_ref, v_ref, o_ref, qf, kf, vf, num, mm, ll, stage, onat, bias, *, seq):
    cp = NSLAB * BLK

    def cast_body(i, _):
        t0 = pl.multiple_of(i * cp, cp)
        s0 = pl.multiple_of(i * BLK, BLK)
        for j, (src, dst) in enumerate(((q_ref, qf), (k_ref, kf), (v_ref, vf))):
            stage[j] = src[pl.ds(t0, cp), :].astype(F32)
            for r in range(NSLAB):
                dst[r, pl.ds(s0, BLK), :] = stage[j, pl.ds(r, BLK, stride=NSLAB), :]
        return 0

    lax.fori_loop(0, seq // cp, cast_body, 0)

    qi = lax.broadcasted_iota(jnp.int32, (BLK, 2 * BLK), 0)
    kj = lax.broadcasted_iota(jnp.int32, (BLK, 2 * BLK), 1)

    def put(slot, dist):
        bias[slot] = jnp.where((dist >= 0) & (dist <= BLK), 0.0, NEG).astype(F32)

    put(0, qi + BLK - kj)
    put(1, qi - kj)
    d1 = NSLAB * (qi % PIECE - kj % (2 * PIECE)) + (qi // PIECE - kj // (2 * PIECE))
    put(2, d1 + BLK)
    put(3, d1)

    ones = jnp.ones((2 * BLK, HEAD_DIM), BF16)

    def block(q, k, v, bias_slot):
        s = lax.dot_general(q, k, (((1,), (1,)), ((), ())), preferred_element_type=F32)
        s = s + bias[bias_slot]
        m = jnp.max(s, axis=-1, keepdims=True)
        e = jnp.exp(s - m).astype(BF16)
        pv = jnp.dot(e, jnp.concatenate([v, ones], axis=1), preferred_element_type=F32)
        return pv[:, :HEAD_DIM], jnp.broadcast_to(m, (BLK, HEAD_DIM)), pv[:, HEAD_DIM:]

    def chunk_body(c, _):
        r0 = pl.multiple_of(c * SLAB_CHUNK, SLAB_CHUNK)
        first_chunk = jnp.where(c == 0, 1, 0)

        def d16_body(mres, _):
            k0 = jnp.maximum(r0 - SLAB_CHUNK, 0) + mres
            for r in range(NSLAB):
                q = qf[r, pl.ds(r0 + mres, BLK, stride=NSLAB), :].astype(BF16)
                k = kf[r, pl.ds(k0, 2 * BLK, stride=NSLAB), :].astype(BF16)
                v = vf[r, pl.ds(k0, 2 * BLK, stride=NSLAB), :].astype(BF16)
                pv, m, l = block(q, k, v, first_chunk)
                dst = pl.ds(mres, BLK, stride=NSLAB)
                num[r, dst, :] = pv
                mm[r, dst, :] = m
                ll[r, dst, :] = l
            return 0

        lax.fori_loop(0, NSLAB, d16_body, 0, unroll=4)

        def d4_body(nb, _):
            loc = pl.multiple_of(nb * BLK, BLK)
            q0 = r0 + loc
            first = jnp.where(q0 == 0, 1, 0)
            k0 = pl.multiple_of(jnp.maximum(q0 - BLK, 0), BLK)
            for r in range(NSLAB):
                q = qf[r, pl.ds(q0, BLK), :].astype(BF16)
                k = kf[r, pl.ds(k0, 2 * BLK), :].astype(BF16)
                v = vf[r, pl.ds(k0, 2 * BLK), :].astype(BF16)
                pv, m, l = block(q, k, v, first)
                dst = pl.ds(loc, BLK)
                num[NSLAB + r, dst, :] = pv
                mm[NSLAB + r, dst, :] = m
                ll[NSLAB + r, dst, :] = l
            return 0

        lax.fori_loop(0, SLAB_CHUNK // BLK, d4_body, 0, unroll=4)

        def d1_body(nb, _):
            loc = pl.multiple_of(nb * PIECE, PIECE)
            q0 = r0 + loc
            first = jnp.where(q0 == 0, 1, 0)
            k0 = pl.multiple_of(jnp.maximum(q0 - PIECE, 0), PIECE)
            gather = lambda ref, start, n: jnp.concatenate(
                [ref[r, pl.ds(start, n), :] for r in range(NSLAB)], axis=0).astype(BF16)
            q = gather(qf, q0, PIECE)
            k = gather(kf, k0, 2 * PIECE)
            v = gather(vf, k0, 2 * PIECE)
            pv, m, l = block(q, k, v, 2 + first)
            for r in range(NSLAB):
                dst = pl.ds(loc, PIECE)
                src = slice(r * PIECE, (r + 1) * PIECE)
                num[2 * NSLAB + r, dst, :] = pv[src]
                mm[2 * NSLAB + r, dst, :] = m[src]
                ll[2 * NSLAB + r, dst, :] = l[src]
            return 0

        lax.fori_loop(0, SLAB_CHUNK // PIECE, d1_body, 0, unroll=16)

        def comb_body(g, _):
            loc = pl.multiple_of(g * BLK, BLK)
            rows = pl.ds(loc, BLK)
            for r in range(NSLAB):
                m0, m1, m2 = mm[r, rows, :], mm[NSLAB + r, rows, :], mm[2 * NSLAB + r, rows, :]
                mx = jnp.maximum(jnp.maximum(m0, m1), m2)
                a0, a1, a2 = jnp.exp(m0 - mx), jnp.exp(m1 - mx), jnp.exp(m2 - mx)
                den = a0 * ll[r, rows, :] + a1 * ll[NSLAB + r, rows, :] + a2 * ll[2 * NSLAB + r, rows, :]
                o = (a0 * num[r, rows, :] + a1 * num[NSLAB + r, rows, :]
                     + a2 * num[2 * NSLAB + r, rows, :]) / den
                onat[pl.ds(r, BLK, stride=NSLAB), :] = o
            t0 = pl.multiple_of(c * CHUNK + g * (NSLAB * BLK), NSLAB * BLK)
            o_ref[pl.ds(t0, NSLAB * BLK), :] = onat[...].astype(o_ref.dtype)
            return 0

        lax.fori_loop(0, SLAB_CHUNK // BLK, comb_body, 0)
        return 0

    lax.fori_loop(0, seq // CHUNK, chunk_body, 0)


def _attention(z3):
    B, S, _ = z3.shape
    rows4 = S // NSLAB
    blk = lambda off: pl.BlockSpec((None, S, HEAD_DIM), lambda b, h, off=off: (b, 0, off + h))
    return pl.pallas_call(
        functools.partial(_attn_kernel, seq=S),
        grid=(B, N_HEADS),
        in_specs=[blk(0), blk(N_HEADS), blk(2 * N_HEADS)],
        out_specs=pl.BlockSpec((None, S, HEAD_DIM), lambda b, h: (b, 0, h)),
        out_shape=jax.ShapeDtypeStruct((B, S, D_ATTN), BF16),
        scratch_shapes=[
            pltpu.VMEM((NSLAB, rows4, HEAD_DIM), F32),
            pltpu.VMEM((NSLAB, rows4, HEAD_DIM), F32),
            pltpu.VMEM((NSLAB, rows4, HEAD_DIM), F32),
            pltpu.VMEM((3 * NSLAB, SLAB_CHUNK, HEAD_DIM), F32),
            pltpu.VMEM((3 * NSLAB, SLAB_CHUNK, HEAD_DIM), F32),
            pltpu.VMEM((3 * NSLAB, SLAB_CHUNK, HEAD_DIM), F32),
            pltpu.VMEM((3, NSLAB * BLK, HEAD_DIM), F32),
            pltpu.VMEM((NSLAB * BLK, HEAD_DIM), F32),
            pltpu.VMEM((4, BLK, 2 * BLK), F32),
        ],
        compiler_params=_cparams(("parallel", "parallel")),
        name="dilated_attn",
    )(z3, z3, z3)


SUBLANES = 8
LANES = 128
HALO = 32
CONV_ROWS = 64
NORM_ROWS = 16


def _fused_in_kernel(x_ref, g_ref, w_ref, wdw_ref, bdw_ref, gln_ref, bln_ref, zq_ref, oc_ref,
                     a_ref, zs, ubuf, yrow, *, n_tiles, tiles_per_seq, groups_per_step):
    i, j = pl.program_id(0), pl.program_id(1)
    nj = pl.num_programs(1)
    tm, C = zs.shape
    cur, prev = i % 2, (i + 1) % 2
    off = HALO - (CONV_WIDTH - 1)
    win_rows = CONV_ROWS + HALO

    @pl.when((i == 0) & (j == 0))
    def _():
        ubuf[...] = jnp.zeros_like(ubuf)

    @pl.when(j == 0)
    def _():
        a_ref[...] = _rms(x_ref[...], g_ref[...]).astype(BF16)

    n_blocks = groups_per_step * (C // LANES)

    def conv_block(it):
        k = it // (C // LANES)
        c0 = (it % (C // LANES)) * LANES
        if not isinstance(it, int):
            c0 = pl.multiple_of(c0, LANES)
        t0 = pl.multiple_of((j * groups_per_step + k) * CONV_ROWS, CONV_ROWS)
        win = ubuf[prev, pl.ds(t0, win_rows), pl.ds(c0, LANES)]
        acc = jnp.zeros((CONV_ROWS, LANES), F32)
        for s in range(SUBLANES):
            ys = win if s == 0 else pltpu.roll(win, win_rows - s, axis=0)
            for blk8 in range(HALO // SUBLANES + 1):
                tap = SUBLANES * blk8 + s - off
                if 0 <= tap < CONV_WIDTH:
                    acc = acc + (ys[SUBLANES * blk8:SUBLANES * blk8 + CONV_ROWS]
                                 * wdw_ref[pl.ds(tap, 1), pl.ds(c0, LANES)])
        yrow[k, :, pl.ds(c0, LANES)] = acc

    def norm_rows():
        for k in range(groups_per_step):
            t0 = pl.multiple_of((j * groups_per_step + k) * CONV_ROWS, CONV_ROWS)
            for r in range(0, CONV_ROWS, NORM_ROWS):
                y = yrow[k, r:r + NORM_ROWS, :] + bdw_ref[...]
                mu = jnp.mean(y, axis=-1, keepdims=True)
                yc = y - mu
                var = jnp.mean(yc * yc, axis=-1, keepdims=True)
                yn = yc * lax.rsqrt(var + EPS) * gln_ref[...] + bln_ref[...]
                oc_ref[pl.ds(t0 + r, NORM_ROWS), :] = (yn * jax.nn.sigmoid(yn)).astype(oc_ref.dtype)

    @pl.when((j < nj - 1) & (i < n_tiles))
    def _():
        z = jnp.dot(a_ref[...], w_ref[...], preferred_element_type=F32)
        zq_ref[...] = (z * jnp.where(j == 0, HEAD_DIM ** -0.5, 1.0)).astype(zq_ref.dtype)
        zs[...] = z
        for it in range(n_blocks):
            conv_block(it)
        norm_rows()

    @pl.when((j < nj - 1) & (i == n_tiles))
    def _():
        def body(it, _):
            conv_block(it)
            return 0

        lax.fori_loop(0, n_blocks, body, 0)
        norm_rows()

    @pl.when((j == nj - 1) & (i < n_tiles))
    def _():
        gate = jnp.dot(a_ref[...], w_ref[...], preferred_element_type=F32)
        first = (i % tiles_per_seq) == 0
        ubuf[cur, pl.ds(0, HALO), :] = jnp.where(first, 0.0, ubuf[prev, pl.ds(tm, HALO), :])
        ubuf[cur, pl.ds(HALO, tm), :] = zs[...] * jax.nn.sigmoid(gate)


def _fused_in(x2, g, w, w_dw, b_dw, g_ln, b_ln, seq, tm=512):
    M, D = x2.shape
    N = w.shape[1]
    C = w_dw.shape[1]
    tn = D_ATTN
    nj = N // tn
    n_tiles = M // tm
    groups = tm // CONV_ROWS
    assert C == tn and N == nj * tn and groups % (nj - 1) == 0 and seq % tm == 0
    last = n_tiles - 1
    vec = pl.BlockSpec((1, C), lambda i, j: (0, 0))
    return pl.pallas_call(
        functools.partial(_fused_in_kernel, n_tiles=n_tiles, tiles_per_seq=seq // tm,
                          groups_per_step=groups // (nj - 1)),
        grid=(n_tiles + 1, nj),
        in_specs=[
            pl.BlockSpec((tm, D), lambda i, j: (jnp.minimum(i, last), 0)),
            pl.BlockSpec((1, D), lambda i, j: (0, 0)),
            pl.BlockSpec((D, tn), lambda i, j: (0, j)),
            pl.BlockSpec((CONV_WIDTH, C), lambda i, j: (0, 0)),
            vec, vec, vec,
        ],
        out_specs=[
            pl.BlockSpec((tm, tn), lambda i, j: (jnp.minimum(i, last),
                                                 jnp.where(i > last, nj - 2, jnp.minimum(j, nj - 2)))),
            pl.BlockSpec((tm, C), lambda i, j: (jnp.maximum(i - 1, 0), 0)),
        ],
        out_shape=[jax.ShapeDtypeStruct((M, (nj - 1) * tn), BF16), jax.ShapeDtypeStruct((M, C), BF16)],
        scratch_shapes=[
            pltpu.VMEM((tm, D), BF16),
            pltpu.VMEM((tm, C), F32),
            pltpu.VMEM((2, HALO + tm, C), F32),
            pltpu.VMEM((groups // (nj - 1), CONV_ROWS, C), F32),
        ],
        compiler_params=_cparams(("arbitrary", "arbitrary")),
        name="in_proj_conv",
    )(x2, g, w, w_dw, b_dw, g_ln, b_ln)


def _out_proj_kernel(oa_ref, oc_ref, x_ref, wa_ref, wc_ref, g_ref, h_ref, f_ref):
    acc = jnp.dot(oa_ref[...], wa_ref[...], preferred_element_type=F32)
    acc = acc + jnp.dot(oc_ref[...], wc_ref[...], preferred_element_type=F32)
    h = x_ref[...] + acc
    h_ref[...] = h
    f_ref[...] = _rms(h, g_ref[...]).astype(f_ref.dtype)


def _out_proj(oa, oc, x2, w_out, g, tm=512):
    M, D = x2.shape
    Ka, Kc = oa.shape[1], oc.shape[1]
    row = lambda n: pl.BlockSpec((tm, n), lambda i: (i, 0))
    return pl.pallas_call(
        _out_proj_kernel,
        grid=(M // tm,),
        in_specs=[
            row(Ka), row(Kc), row(D),
            pl.BlockSpec((Ka, D), lambda i: (0, 0)),
            pl.BlockSpec((Kc, D), lambda i: (Ka // Kc, 0)),
            pl.BlockSpec((1, D), lambda i: (0, 0)),
        ],
        out_specs=[row(D), row(D)],
        out_shape=[jax.ShapeDtypeStruct((M, D), F32), jax.ShapeDtypeStruct((M, D), BF16)],
        compiler_params=_cparams(("parallel",)),
        name="out_proj",
    )(oa, oc, x2, w_out, w_out, g)


def _ffn_kernel(f_ref, wg_ref, wu_ref, wd_ref, o_ref):
    @pl.when(pl.program_id(1) == 0)
    def _():
        o_ref[...] = jnp.zeros_like(o_ref)

    f = f_ref[...]
    g = jnp.dot(f, wg_ref[...], preferred_element_type=F32)
    u = jnp.dot(f, wu_ref[...], preferred_element_type=F32)
    act = (g * jax.nn.sigmoid(g) * u).astype(BF16)
    o_ref[...] += jnp.dot(act, wd_ref[...], preferred_element_type=F32)


def _ffn(f, wg, wu, wd, tm=1024, tf=512):
    M, D = f.shape
    F = wg.shape[1]
    return pl.pallas_call(
        _ffn_kernel,
        grid=(M // tm, F // tf),
        in_specs=[
            pl.BlockSpec((tm, D), lambda i, j: (i, 0)),
            pl.BlockSpec((D, tf), lambda i, j: (0, j)),
            pl.BlockSpec((D, tf), lambda i, j: (0, j)),
            pl.BlockSpec((tf, D), lambda i, j: (j, 0)),
        ],
        out_specs=pl.BlockSpec((tm, D), lambda i, j: (i, 0)),
        out_shape=jax.ShapeDtypeStruct((M, D), F32),
        compiler_params=_cparams(("parallel", "arbitrary")),
        name="ffn",
    )(f, wg, wu, wd)


def _ple_kernel(h_ref, d_ref, p_ref, wpg_ref, bpg_ref, wple_ref, gp_ref, gf_ref, o_ref, *, final):
    h2 = h_ref[...] + d_ref[...]
    a = _rms(h2, gp_ref[...]).astype(BF16)
    gte = jax.nn.sigmoid(jnp.dot(a, wpg_ref[...], preferred_element_type=F32) + bpg_ref[...])
    ple = jnp.dot(p_ref[...].astype(BF16), wple_ref[...], preferred_element_type=F32)
    h3 = h2 + ple * gte
    o_ref[...] = _rms(h3, gf_ref[...]) if final else h3


def _ple(h1, d, p2, wpg, bpg, wple, g_ple, g_final, final, tm=512):
    M, D = h1.shape
    P = p2.shape[1]
    row = lambda n: pl.BlockSpec((tm, n), lambda i: (i, 0))
    full = lambda r, c: pl.BlockSpec((r, c), lambda i: (0, 0))
    return pl.pallas_call(
        functools.partial(_ple_kernel, final=final),
        grid=(M // tm,),
        in_specs=[row(D), row(D), row(P), full(D, D), full(1, D), full(P, D), full(1, D), full(1, D)],
        out_specs=row(D),
        out_shape=jax.ShapeDtypeStruct((M, D), F32),
        compiler_params=_cparams(("parallel",)),
        name="ple_final",
    )(h1, d, p2, wpg, bpg, wple, g_ple, g_final)


def kernel(x, p, g_mix, w_in, w_dw, b_dw, g_conv_ln, b_conv_ln, w_out, g_ffn, w_gate, w_up, w_down,
           g_ple, w_pgate, b_pgate, w_ple, g_final):
    B, S, D = x.shape
    depth = w_in.shape[0]
    M = B * S
    h = x.reshape(M, D)
    out = None
    for i in range(depth):
        last = i == depth - 1
        zq, oc = _fused_in(h, g_mix[i][None], w_in[i].astype(BF16), w_dw[i, :, 0, :], b_dw[i][None],
                           g_conv_ln[i][None], b_conv_ln[i][None], S)
        oa = _attention(zq.reshape(B, S, -1)).reshape(M, -1)
        h1, f = _out_proj(oa, oc, h, w_out[i].astype(BF16), g_ffn[i][None])
        d = _ffn(f, w_gate[i].astype(BF16), w_up[i].astype(BF16), w_down[i].astype(BF16))
        out = _ple(h1, d, p[i].reshape(M, -1), w_pgate[i].astype(BF16), b_pgate[i][None],
                   w_ple[i].astype(BF16), g_ple[i][None], g_final[None], last)
        h = out
    return out.reshape(B, S, D)
```

```python
import functools

import jax
import jax.numpy as jnp
from jax import lax
from jax.experimental import pallas as pl
from jax.experimental.pallas import tpu as pltpu

F32 = jnp.float32
BF16 = jnp.bfloat16

EPS = 1e-6
N_HEADS = 8
HEAD_DIM = 128
D_ATTN = N_HEADS * HEAD_DIM
BLK = 128
DILATIONS = (1, 4, 16)
CONV_WIDTH = 31
CHUNK = DILATIONS[-1] * BLK
NEG = -1e30

VMEM_LIMIT = 56 * 1024 * 1024


def _cparams(sem):
    return pltpu.CompilerParams(dimension_semantics=sem, vmem_limit_bytes=VMEM_LIMIT)


def _rms(x, g):
    return x * lax.rsqrt(jnp.mean(x * x, axis=-1, keepdims=True) + EPS) * g

NSLAB = 4
SLAB_CHUNK = CHUNK // NSLAB
PIECE = BLK // NSLAB


def _attn_kernel(q_ref, k_ref, v_ref, o_ref, qf, kf, vf, num, mm, ll, stage, onat, bias, *, seq):
    cp = NSLAB * BLK

    def cast_body(i, _):
        t0 = pl.multiple_of(i * cp, cp)
        s0 = pl.multiple_of(i * BLK, BLK)
        for j, (src, dst) in enumerate(((q_ref, qf), (k_ref, kf), (v_ref, vf))):
            stage[j] = src[pl.ds(t0, cp), :].astype(F32)
            for r in range(NSLAB):
                dst[r, pl.ds(s0, BLK), :] = stage[j, pl.ds(r, BLK, stride=NSLAB), :]
        return 0

    lax.fori_loop(0, seq // cp, cast_body, 0)

    qi = lax.broadcasted_iota(jnp.int32, (BLK, 2 * BLK), 0)
    kj = lax.broadcasted_iota(jnp.int32, (BLK, 2 * BLK), 1)

    def put(slot, dist):
        bias[slot] = jnp.where((dist >= 0) & (dist <= BLK), 0.0, NEG).astype(F32)

    put(0, qi + BLK - kj)
    put(1, qi - kj)
    d1 = NSLAB * (qi % PIECE - kj % (2 * PIECE)) + (qi // PIECE - kj // (2 * PIECE))
    put(2, d1 + BLK)
    put(3, d1)

    ones = jnp.ones((2 * BLK, HEAD_DIM), BF16)

    def block(q, k, v, bias_slot):
        s = lax.dot_general(q, k, (((1,), (1,)), ((), ())), preferred_element_type=F32)
        s = s + bias[bias_slot]
        m = jnp.max(s, axis=-1, keepdims=True)
        e = jnp.exp(s - m).astype(BF16)
        pv = jnp.dot(e, jnp.concatenate([v, ones], axis=1), preferred_element_type=F32)
        return pv[:, :HEAD_DIM], jnp.broadcast_to(m, (BLK, HEAD_DIM)), pv[:, HEAD_DIM:]

    def chunk_body(c, _):
        r0 = pl.multiple_of(c * SLAB_CHUNK, SLAB_CHUNK)
        first_chunk = jnp.where(c == 0, 1, 0)

        def d16_body(mres, _):
            k0 = jnp.maximum(r0 - SLAB_CHUNK, 0) + mres
            for r in range(NSLAB):
                q = qf[r, pl.ds(r0 + mres, BLK, stride=NSLAB), :].astype(BF16)
                k = kf[r, pl.ds(k0, 2 * BLK, stride=NSLAB), :].astype(BF16)
                v = vf[r, pl.ds(k0, 2 * BLK, stride=NSLAB), :].astype(BF16)
                pv, m, l = block(q, k, v, first_chunk)
                dst = pl.ds(mres, BLK, stride=NSLAB)
                num[r, dst, :] = pv
                mm[r, dst, :] = m
                ll[r, dst, :] = l
            return 0

        lax.fori_loop(0, NSLAB, d16_body, 0, unroll=4)

        def d4_body(nb, _):
            loc = pl.multiple_of(nb * BLK, BLK)
            q0 = r0 + loc
            first = jnp.where(q0 == 0, 1, 0)
            k0 = pl.multiple_of(jnp.maximum(q0 - BLK, 0), BLK)
            for r in range(NSLAB):
                q = qf[r, pl.ds(q0, BLK), :].astype(BF16)
                k = kf[r, pl.ds(k0, 2 * BLK), :].astype(BF16)
                v = vf[r, pl.ds(k0, 2 * BLK), :].astype(BF16)
                pv, m, l = block(q, k, v, first)
                dst = pl.ds(loc, BLK)
                num[NSLAB + r, dst, :] = pv
                mm[NSLAB + r, dst, :] = m
                ll[NSLAB + r, dst, :] = l
            return 0

        lax.fori_loop(0, SLAB_CHUNK // BLK, d4_body, 0, unroll=4)

        def d1_body(nb, _):
            loc = pl.multiple_of(nb * PIECE, PIECE)
            q0 = r0 + loc
            first = jnp.where(q0 == 0, 1, 0)
            k0 = pl.multiple_of(jnp.maximum(q0 - PIECE, 0), PIECE)
            gather = lambda ref, start, n: jnp.concatenate(
                [ref[r, pl.ds(start, n), :] for r in range(NSLAB)], axis=0).astype(BF16)
            q = gather(qf, q0, PIECE)
            k = gather(kf, k0, 2 * PIECE)
            v = gather(vf, k0, 2 * PIECE)
            pv, m, l = block(q, k, v, 2 + first)
            for r in range(NSLAB):
                dst = pl.ds(loc, PIECE)
                src = slice(r * PIECE, (r + 1) * PIECE)
                num[2 * NSLAB + r, dst, :] = pv[src]
                mm[2 * NSLAB + r, dst, :] = m[src]
                ll[2 * NSLAB + r, dst, :] = l[src]
            return 0

        lax.fori_loop(0, SLAB_CHUNK // PIECE, d1_body, 0, unroll=16)

        def comb_body(g, _):
            loc = pl.multiple_of(g * BLK, BLK)
            rows = pl.ds(loc, BLK)
            for r in range(NSLAB):
                m0, m1, m2 = mm[r, rows, :], mm[NSLAB + r, rows, :], mm[2 * NSLAB + r, rows, :]
                mx = jnp.maximum(jnp.maximum(m0, m1), m2)
                a0, a1, a2 = jnp.exp(m0 - mx), jnp.exp(m1 - mx), jnp.exp(m2 - mx)
                den = a0 * ll[r, rows, :] + a1 * ll[NSLAB + r, rows, :] + a2 * ll[2 * NSLAB + r, rows, :]
                o = (a0 * num[r, rows, :] + a1 * num[NSLAB + r, rows, :]
                     + a2 * num[2 * NSLAB + r, rows, :]) / den
                onat[pl.ds(r, BLK, stride=NSLAB), :] = o
            t0 = pl.multiple_of(c * CHUNK + g * (NSLAB * BLK), NSLAB * BLK)
            o_ref[pl.ds(t0, NSLAB * BLK), :] = onat[...].astype(o_ref.dtype)
            return 0

        lax.fori_loop(0, SLAB_CHUNK // BLK, comb_body, 0)
        return 0

    lax.fori_loop(0, seq // CHUNK, chunk_body, 0)


def _attention(z3):
    B, S, _ = z3.shape
    rows4 = S // NSLAB
    blk = lambda off: pl.BlockSpec((None, S, HEAD_DIM), lambda b, h, off=off: (b, 0, off + h))
    return pl.pallas_call(
        functools.partial(_attn_kernel, seq=S),
        grid=(B, N_HEADS),
        in_specs=[blk(0), blk(N_HEADS), blk(2 * N_HEADS)],
        out_specs=pl.BlockSpec((None, S, HEAD_DIM), lambda b, h: (b, 0, h)),
        out_shape=jax.ShapeDtypeStruct((B, S, D_ATTN), BF16),
        scratch_shapes=[
            pltpu.VMEM((NSLAB, rows4, HEAD_DIM), F32),
            pltpu.VMEM((NSLAB, rows4, HEAD_DIM), F32),
            pltpu.VMEM((NSLAB, rows4, HEAD_DIM), F32),
            pltpu.VMEM((3 * NSLAB, SLAB_CHUNK, HEAD_DIM), F32),
            pltpu.VMEM((3 * NSLAB, SLAB_CHUNK, HEAD_DIM), F32),
            pltpu.VMEM((3 * NSLAB, SLAB_CHUNK, HEAD_DIM), F32),
            pltpu.VMEM((3, NSLAB * BLK, HEAD_DIM), F32),
            pltpu.VMEM((NSLAB * BLK, HEAD_DIM), F32),
            pltpu.VMEM((4, BLK, 2 * BLK), F32),
        ],
        compiler_params=_cparams(("parallel", "parallel")),
        name="dilated_attn",
    )(z3, z3, z3)


SUBLANES = 8
LANES = 128
HALO = 32
CONV_ROWS = 64
NORM_ROWS = 16


def _fused_in_kernel(*refs, n_tiles, tiles_per_seq, groups_per_step, n_cast):
    x_ref, g_ref, w_ref, wdw_ref, bdw_ref, gln_ref, bln_ref = refs[:7]
    cast_src = refs[7:7 + n_cast]
    zq_ref, oc_ref = refs[7 + n_cast:9 + n_cast]
    cast_dst = refs[9 + n_cast:9 + 2 * n_cast]
    a_ref, zs, ubuf, yrow = refs[9 + 2 * n_cast:]
    i, j = pl.program_id(0), pl.program_id(1)
    nj = pl.num_programs(1)
    tm, C = zs.shape
    cur, prev = i % 2, (i + 1) % 2
    off = HALO - (CONV_WIDTH - 1)
    win_rows = CONV_ROWS + HALO

    @pl.when((i == 0) & (j == 0))
    def _():
        ubuf[...] = jnp.zeros_like(ubuf)

    @pl.when(j == 0)
    def _():
        a_ref[...] = _rms(x_ref[...], g_ref[...]).astype(BF16)

    n_blocks = groups_per_step * (C // LANES)

    def conv_block(it):
        k = it // (C // LANES)
        c0 = (it % (C // LANES)) * LANES
        if not isinstance(it, int):
            c0 = pl.multiple_of(c0, LANES)
        t0 = pl.multiple_of((j * groups_per_step + k) * CONV_ROWS, CONV_ROWS)
        win = ubuf[prev, pl.ds(t0, win_rows), pl.ds(c0, LANES)]
        acc = jnp.zeros((CONV_ROWS, LANES), F32)
        for s in range(SUBLANES):
            ys = win if s == 0 else pltpu.roll(win, win_rows - s, axis=0)
            for blk8 in range(HALO // SUBLANES + 1):
                tap = SUBLANES * blk8 + s - off
                if 0 <= tap < CONV_WIDTH:
                    acc = acc + (ys[SUBLANES * blk8:SUBLANES * blk8 + CONV_ROWS]
                                 * wdw_ref[pl.ds(tap, 1), pl.ds(c0, LANES)])
        yrow[k, :, pl.ds(c0, LANES)] = acc

    def norm_rows():
        for k in range(groups_per_step):
            t0 = pl.multiple_of((j * groups_per_step + k) * CONV_ROWS, CONV_ROWS)
            for r in range(0, CONV_ROWS, NORM_ROWS):
                y = yrow[k, r:r + NORM_ROWS, :] + bdw_ref[...]
                mu = jnp.mean(y, axis=-1, keepdims=True)
                yc = y - mu
                var = jnp.mean(yc * yc, axis=-1, keepdims=True)
                yn = yc * lax.rsqrt(var + EPS) * gln_ref[...] + bln_ref[...]
                oc_ref[pl.ds(t0 + r, NORM_ROWS), :] = (yn * jax.nn.sigmoid(yn)).astype(oc_ref.dtype)

    @pl.when((j < nj - 1) & (i < n_tiles))
    def _():
        z = jnp.dot(a_ref[...], w_ref[...], preferred_element_type=F32)
        zq_ref[...] = (z * jnp.where(j == 0, HEAD_DIM ** -0.5, 1.0)).astype(zq_ref.dtype)
        zs[...] = z
        for it in range(n_blocks):
            conv_block(it)
        norm_rows()

    @pl.when((j < nj - 1) & (i == n_tiles))
    def _():
        def body(it, _):
            conv_block(it)
            return 0

        lax.fori_loop(0, n_blocks, body, 0)
        norm_rows()

    @pl.when((j == nj - 1) & (i < n_tiles))
    def _():
        gate = jnp.dot(a_ref[...], w_ref[...], preferred_element_type=F32)
        first = (i % tiles_per_seq) == 0
        ubuf[cur, pl.ds(0, HALO), :] = jnp.where(first, 0.0, ubuf[prev, pl.ds(tm, HALO), :])
        ubuf[cur, pl.ds(HALO, tm), :] = zs[...] * jax.nn.sigmoid(gate)
        for src, dst in zip(cast_src, cast_dst):
            dst[...] = src[...].astype(dst.dtype)


def _fused_in(x2, g, w, w_dw, b_dw, g_ln, b_ln, seq, cast_weights, tm=512):
    M, D = x2.shape
    N = w.shape[1]
    C = w_dw.shape[1]
    tn = D_ATTN
    nj = N // tn
    n_tiles = M // tm
    groups = tm // CONV_ROWS
    assert C == tn and N == nj * tn and groups % (nj - 1) == 0 and seq % tm == 0
    last = n_tiles - 1
    vec = pl.BlockSpec((1, C), lambda i, j: (0, 0))
    slab = lambda wt: pl.BlockSpec((wt.shape[0] // n_tiles, wt.shape[1]), lambda i, j: (jnp.minimum(i, last), 0))
    assert all(wt.shape[0] % (16 * n_tiles) == 0 for wt in cast_weights)
    return pl.pallas_call(
        functools.partial(_fused_in_kernel, n_tiles=n_tiles, tiles_per_seq=seq // tm,
                          groups_per_step=groups // (nj - 1), n_cast=len(cast_weights)),
        grid=(n_tiles + 1, nj),
        in_specs=[
            pl.BlockSpec((tm, D), lambda i, j: (jnp.minimum(i, last), 0)),
            pl.BlockSpec((1, D), lambda i, j: (0, 0)),
            pl.BlockSpec((D, tn), lambda i, j: (0, j)),
            pl.BlockSpec((CONV_WIDTH, C), lambda i, j: (0, 0)),
            vec, vec, vec,
        ] + [slab(wt) for wt in cast_weights],
        out_specs=[
            pl.BlockSpec((tm, tn), lambda i, j: (jnp.minimum(i, last),
                                                 jnp.where(i > last, nj - 2, jnp.minimum(j, nj - 2)))),
            pl.BlockSpec((tm, C), lambda i, j: (jnp.maximum(i - 1, 0), 0)),
        ] + [slab(wt) for wt in cast_weights],
        out_shape=[jax.ShapeDtypeStruct((M, (nj - 1) * tn), BF16), jax.ShapeDtypeStruct((M, C), BF16)]
        + [jax.ShapeDtypeStruct(wt.shape, BF16) for wt in cast_weights],
        scratch_shapes=[
            pltpu.VMEM((tm, D), BF16),
            pltpu.VMEM((tm, C), F32),
            pltpu.VMEM((2, HALO + tm, C), F32),
            pltpu.VMEM((groups // (nj - 1), CONV_ROWS, C), F32),
        ],
        compiler_params=_cparams(("arbitrary", "arbitrary")),
        name="in_proj_conv",
    )(x2, g, w, w_dw, b_dw, g_ln, b_ln, *cast_weights)


def _out_proj_kernel(oa_ref, oc_ref, x_ref, wa_ref, wc_ref, g_ref, h_ref, f_ref):
    acc = jnp.dot(oa_ref[...], wa_ref[...], preferred_element_type=F32)
    acc = acc + jnp.dot(oc_ref[...], wc_ref[...], preferred_element_type=F32)
    h = x_ref[...] + acc
    h_ref[...] = h
    f_ref[...] = _rms(h, g_ref[...]).astype(f_ref.dtype)


def _out_proj(oa, oc, x2, w_out, g, tm=512):
    M, D = x2.shape
    Ka, Kc = oa.shape[1], oc.shape[1]
    row = lambda n: pl.BlockSpec((tm, n), lambda i: (i, 0))
    return pl.pallas_call(
        _out_proj_kernel,
        grid=(M // tm,),
        in_specs=[
            row(Ka), row(Kc), row(D),
            pl.BlockSpec((Ka, D), lambda i: (0, 0)),
            pl.BlockSpec((Kc, D), lambda i: (Ka // Kc, 0)),
            pl.BlockSpec((1, D), lambda i: (0, 0)),
        ],
        out_specs=[row(D), row(D)],
        out_shape=[jax.ShapeDtypeStruct((M, D), F32), jax.ShapeDtypeStruct((M, D), BF16)],
        compiler_params=_cparams(("parallel",)),
        name="out_proj",
    )(oa, oc, x2, w_out, w_out, g)


def _ffn_kernel(f_ref, wg_ref, wu_ref, wd_ref, o_ref):
    @pl.when(pl.program_id(1) == 0)
    def _():
        o_ref[...] = jnp.zeros_like(o_ref)

    f = f_ref[...]
    g = jnp.dot(f, wg_ref[...], preferred_element_type=F32)
    u = jnp.dot(f, wu_ref[...], preferred_element_type=F32)
    act = (g * jax.nn.sigmoid(g) * u).astype(BF16)
    o_ref[...] += jnp.dot(act, wd_ref[...], preferred_element_type=F32)


def _ffn(f, wg, wu, wd, tm=1024, tf=512):
    M, D = f.shape
    F = wg.shape[1]
    return pl.pallas_call(
        _ffn_kernel,
        grid=(M // tm, F // tf),
        in_specs=[
            pl.BlockSpec((tm, D), lambda i, j: (i, 0)),
            pl.BlockSpec((D, tf), lambda i, j: (0, j)),
            pl.BlockSpec((D, tf), lambda i, j: (0, j)),
            pl.BlockSpec((tf, D), lambda i, j: (j, 0)),
        ],
        out_specs=pl.BlockSpec((tm, D), lambda i, j: (i, 0)),
        out_shape=jax.ShapeDtypeStruct((M, D), F32),
        compiler_params=_cparams(("parallel", "arbitrary")),
        name="ffn",
    )(f, wg, wu, wd)


def _ple_kernel(h_ref, d_ref, p_ref, wpg_ref, bpg_ref, wple_ref, gp_ref, gf_ref, o_ref, *, final):
    h2 = h_ref[...] + d_ref[...]
    a = _rms(h2, gp_ref[...]).astype(BF16)
    gte = jax.nn.sigmoid(jnp.dot(a, wpg_ref[...], preferred_element_type=F32) + bpg_ref[...])
    ple = jnp.dot(p_ref[...].astype(BF16), wple_ref[...], preferred_element_type=F32)
    h3 = h2 + ple * gte
    o_ref[...] = _rms(h3, gf_ref[...]) if final else h3


def _ple(h1, d, p2, wpg, bpg, wple, g_ple, g_final, final, tm=512):
    M, D = h1.shape
    P = p2.shape[1]
    row = lambda n: pl.BlockSpec((tm, n), lambda i: (i, 0))
    full = lambda r, c: pl.BlockSpec((r, c), lambda i: (0, 0))
    return pl.pallas_call(
        functools.partial(_ple_kernel, final=final),
        grid=(M // tm,),
        in_specs=[row(D), row(D), row(P), full(D, D), full(1, D), full(P, D), full(1, D), full(1, D)],
        out_specs=row(D),
        out_shape=jax.ShapeDtypeStruct((M, D), F32),
        compiler_params=_cparams(("parallel",)),
        name="ple_final",
    )(h1, d, p2, wpg, bpg, wple, g_ple, g_final)


def kernel(x, p, g_mix, w_in, w_dw, b_dw, g_conv_ln, b_conv_ln, w_out, g_ffn, w_gate, w_up, w_down,
           g_ple, w_pgate, b_pgate, w_ple, g_final):
    B, S, D = x.shape
    depth = w_in.shape[0]
    M = B * S
    h = x.reshape(M, D)
    out = None
    for i in range(depth):
        last = i == depth - 1
        zq, oc, wo, wg, wu, wd, wpg = _fused_in(
            h, g_mix[i][None], w_in[i].astype(BF16), w_dw[i, :, 0, :], b_dw[i][None], g_conv_ln[i][None],
            b_conv_ln[i][None], S, (w_out[i], w_gate[i], w_up[i], w_down[i], w_pgate[i]))
        oa = _attention(zq.reshape(B, S, -1)).reshape(M, -1)
        h1, f = _out_proj(oa, oc, h, wo, g_ffn[i][None])
        d = _ffn(f, wg, wu, wd)
        out = _ple(h1, d, p[i].reshape(M, -1), wpg, b_pgate[i][None],
                   w_ple[i].astype(BF16), g_ple[i][None], g_final[None], last)
        h = out
    return out.reshape(B, S, D)
```

```python
import functools

import jax
import jax.numpy as jnp
from jax import lax
from jax.experimental import pallas as pl
from jax.experimental.pallas import tpu as pltpu

F32 = jnp.float32
BF16 = jnp.bfloat16

EPS = 1e-6
N_HEADS = 8
HEAD_DIM = 128
D_ATTN = N_HEADS * HEAD_DIM
BLK = 128
DILATIONS = (1, 4, 16)
CONV_WIDTH = 31
CHUNK = DILATIONS[-1] * BLK
NEG = -1e30

VMEM_LIMIT = 56 * 1024 * 1024


def _cparams(sem):
    return pltpu.CompilerParams(dimension_semantics=sem, vmem_limit_bytes=VMEM_LIMIT)


def _rms(x, g):
    return x * lax.rsqrt(jnp.mean(x * x, axis=-1, keepdims=True) + EPS) * g

NSLAB = 4
SLAB_CHUNK = CHUNK // NSLAB
PIECE = BLK // NSLAB


def _attn_kernel(q_ref, k_ref, v_ref, o_ref, qf, kf, vf, num, mm, ll, stage, onat, bias, *, seq):
    cp = NSLAB * BLK

    def cast_body(i, _):
        t0 = pl.multiple_of(i * cp, cp)
        s0 = pl.multiple_of(i * BLK, BLK)
        for j, (src, dst) in enumerate(((q_ref, qf), (k_ref, kf), (v_ref, vf))):
            stage[j] = src[pl.ds(t0, cp), :].astype(F32)
            for r in range(NSLAB):
                dst[r, pl.ds(s0, BLK), :] = stage[j, pl.ds(r, BLK, stride=NSLAB), :]
        return 0

    lax.fori_loop(0, seq // cp, cast_body, 0)

    qi = lax.broadcasted_iota(jnp.int32, (BLK, 2 * BLK), 0)
    kj = lax.broadcasted_iota(jnp.int32, (BLK, 2 * BLK), 1)

    def put(slot, dist):
        bias[slot] = jnp.where((dist >= 0) & (dist <= BLK), 0.0, NEG).astype(F32)

    put(0, qi + BLK - kj)
    put(1, qi - kj)
    d1 = NSLAB * (qi % PIECE - kj % (2 * PIECE)) + (qi // PIECE - kj // (2 * PIECE))
    put(2, d1 + BLK)
    put(3, d1)

    ones = jnp.ones((2 * BLK, HEAD_DIM), BF16)

    def block(q, k, v, bias_slot):
        s = lax.dot_general(q, k, (((1,), (1,)), ((), ())), preferred_element_type=F32)
        s = s + bias[bias_slot]
        m = jnp.max(s, axis=-1, keepdims=True)
        e = jnp.exp(s - m).astype(BF16)
        pv = jnp.dot(e, jnp.concatenate([v, ones], axis=1), preferred_element_type=F32)
        return pv[:, :HEAD_DIM], jnp.broadcast_to(m, (BLK, HEAD_DIM)), pv[:, HEAD_DIM:]

    def chunk_body(c, _):
        r0 = pl.multiple_of(c * SLAB_CHUNK, SLAB_CHUNK)
        first_chunk = jnp.where(c == 0, 1, 0)

        def d16_body(mres, _):
            k0 = jnp.maximum(r0 - SLAB_CHUNK, 0) + mres
            for r in range(NSLAB):
                q = qf[r, pl.ds(r0 + mres, BLK, stride=NSLAB), :].astype(BF16)
                k = kf[r, pl.ds(k0, 2 * BLK, stride=NSLAB), :].astype(BF16)
                v = vf[r, pl.ds(k0, 2 * BLK, stride=NSLAB), :].astype(BF16)
                pv, m, l = block(q, k, v, first_chunk)
                dst = pl.ds(mres, BLK, stride=NSLAB)
                num[r, dst, :] = pv
                mm[r, dst, :] = m
                ll[r, dst, :] = l
            return 0

        lax.fori_loop(0, NSLAB, d16_body, 0, unroll=4)

        def d4_body(nb, _):
            loc = pl.multiple_of(nb * BLK, BLK)
            q0 = r0 + loc
            first = jnp.where(q0 == 0, 1, 0)
            k0 = pl.multiple_of(jnp.maximum(q0 - BLK, 0), BLK)
            for r in range(NSLAB):
                q = qf[r, pl.ds(q0, BLK), :].astype(BF16)
                k = kf[r, pl.ds(k0, 2 * BLK), :].astype(BF16)
                v = vf[r, pl.ds(k0, 2 * BLK), :].astype(BF16)
                pv, m, l = block(q, k, v, first)
                dst = pl.ds(loc, BLK)
                num[NSLAB + r, dst, :] = pv
                mm[NSLAB + r, dst, :] = m
                ll[NSLAB + r, dst, :] = l
            return 0

        lax.fori_loop(0, SLAB_CHUNK // BLK, d4_body, 0, unroll=4)

        def d1_body(nb, _):
            loc = pl.multiple_of(nb * PIECE, PIECE)
            q0 = r0 + loc
            first = jnp.where(q0 == 0, 1, 0)
            k0 = pl.multiple_of(jnp.maximum(q0 - PIECE, 0), PIECE)
            gather = lambda ref, start, n: jnp.concatenate(
                [ref[r, pl.ds(start, n), :] for r in range(NSLAB)], axis=0).astype(BF16)
            q = gather(qf, q0, PIECE)
            k = gather(kf, k0, 2 * PIECE)
            v = gather(vf, k0, 2 * PIECE)
            pv, m, l = block(q, k, v, 2 + first)
            for r in range(NSLAB):
                dst = pl.ds(loc, PIECE)
                src = slice(r * PIECE, (r + 1) * PIECE)
                num[2 * NSLAB + r, dst, :] = pv[src]
                mm[2 * NSLAB + r, dst, :] = m[src]
                ll[2 * NSLAB + r, dst, :] = l[src]
            return 0

        lax.fori_loop(0, SLAB_CHUNK // PIECE, d1_body, 0, unroll=16)

        def comb_body(g, _):
            loc = pl.multiple_of(g * BLK, BLK)
            rows = pl.ds(loc, BLK)
            for r in range(NSLAB):
                m0, m1, m2 = mm[r, rows, :], mm[NSLAB + r, rows, :], mm[2 * NSLAB + r, rows, :]
                mx = jnp.maximum(jnp.maximum(m0, m1), m2)
                a0, a1, a2 = jnp.exp(m0 - mx), jnp.exp(m1 - mx), jnp.exp(m2 - mx)
                den = a0 * ll[r, rows, :] + a1 * ll[NSLAB + r, rows, :] + a2 * ll[2 * NSLAB + r, rows, :]
                o = (a0 * num[r, rows, :] + a1 * num[NSLAB + r, rows, :]
                     + a2 * num[2 * NSLAB + r, rows, :]) / den
                onat[pl.ds(r, BLK, stride=NSLAB), :] = o
            t0 = pl.multiple_of(c * CHUNK + g * (NSLAB * BLK), NSLAB * BLK)
            o_ref[pl.ds(t0, NSLAB * BLK), :] = onat[...].astype(o_ref.dtype)
            return 0

        lax.fori_loop(0, SLAB_CHUNK // BLK, comb_body, 0)
        return 0

    lax.fori_loop(0, seq // CHUNK, chunk_body, 0)


def _attention(z3):
    B, S, _ = z3.shape
    rows4 = S // NSLAB
    blk = lambda off: pl.BlockSpec((None, S, HEAD_DIM), lambda b, h, off=off: (b, 0, off + h))
    return pl.pallas_call(
        functools.partial(_attn_kernel, seq=S),
        grid=(B, N_HEADS),
        in_specs=[blk(0), blk(N_HEADS), blk(2 * N_HEADS)],
        out_specs=pl.BlockSpec((None, S, HEAD_DIM), lambda b, h: (b, 0, h)),
        out_shape=jax.ShapeDtypeStruct((B, S, D_ATTN), BF16),
        scratch_shapes=[
            pltpu.VMEM((NSLAB, rows4, HEAD_DIM), F32),
            pltpu.VMEM((NSLAB, rows4, HEAD_DIM), F32),
            pltpu.VMEM((NSLAB, rows4, HEAD_DIM), F32),
            pltpu.VMEM((3 * NSLAB, SLAB_CHUNK, HEAD_DIM), F32),
            pltpu.VMEM((3 * NSLAB, SLAB_CHUNK, HEAD_DIM), F32),
            pltpu.VMEM((3 * NSLAB, SLAB_CHUNK, HEAD_DIM), F32),
            pltpu.VMEM((3, NSLAB * BLK, HEAD_DIM), F32),
            pltpu.VMEM((NSLAB * BLK, HEAD_DIM), F32),
            pltpu.VMEM((4, BLK, 2 * BLK), F32),
        ],
        compiler_params=_cparams(("parallel", "parallel")),
        name="dilated_attn",
    )(z3, z3, z3)


SUBLANES = 8
LANES = 128
HALO = 32
CONV_ROWS = 64
NORM_ROWS = 16


def _fused_in_kernel(*refs, n_tiles, tiles_per_seq, groups_per_step, n_cast):
    x_ref, g_ref, w_ref, wdw_ref, bdw_ref, gln_ref, bln_ref = refs[:7]
    cast_src = refs[7:7 + n_cast]
    zq_ref, oc_ref = refs[7 + n_cast:9 + n_cast]
    cast_dst = refs[9 + n_cast:9 + 2 * n_cast]
    a_ref, zs, ubuf, yrow = refs[9 + 2 * n_cast:]
    i, j = pl.program_id(0), pl.program_id(1)
    nj = pl.num_programs(1)
    tm, C = zs.shape
    cur, prev = i % 2, (i + 1) % 2
    off = HALO - (CONV_WIDTH - 1)
    win_rows = CONV_ROWS + HALO

    @pl.when((i == 0) & (j == 0))
    def _():
        ubuf[...] = jnp.zeros_like(ubuf)

    @pl.when(j == 0)
    def _():
        a_ref[...] = _rms(x_ref[...], g_ref[...]).astype(BF16)

    n_blocks = groups_per_step * (C // LANES)

    def conv_block(it):
        k = it // (C // LANES)
        c0 = (it % (C // LANES)) * LANES
        if not isinstance(it, int):
            c0 = pl.multiple_of(c0, LANES)
        t0 = pl.multiple_of((j * groups_per_step + k) * CONV_ROWS, CONV_ROWS)
        win = ubuf[prev, pl.ds(t0, win_rows), pl.ds(c0, LANES)]
        acc = jnp.zeros((CONV_ROWS, LANES), F32)
        for s in range(SUBLANES):
            ys = win if s == 0 else pltpu.roll(win, win_rows - s, axis=0)
            for blk8 in range(HALO // SUBLANES + 1):
                tap = SUBLANES * blk8 + s - off
                if 0 <= tap < CONV_WIDTH:
                    acc = acc + (ys[SUBLANES * blk8:SUBLANES * blk8 + CONV_ROWS]
                                 * wdw_ref[pl.ds(tap, 1), pl.ds(c0, LANES)])
        yrow[k, :, pl.ds(c0, LANES)] = acc

    def norm_rows():
        for k in range(groups_per_step):
            t0 = pl.multiple_of((j * groups_per_step + k) * CONV_ROWS, CONV_ROWS)
            for r in range(0, CONV_ROWS, NORM_ROWS):
                y = yrow[k, r:r + NORM_ROWS, :] + bdw_ref[...]
                mu = jnp.mean(y, axis=-1, keepdims=True)
                yc = y - mu
                var = jnp.mean(yc * yc, axis=-1, keepdims=True)
                yn = yc * lax.rsqrt(var + EPS) * gln_ref[...] + bln_ref[...]
                oc_ref[pl.ds(t0 + r, NORM_ROWS), :] = (yn * jax.nn.sigmoid(yn)).astype(oc_ref.dtype)

    @pl.when((j < nj - 1) & (i < n_tiles))
    def _():
        z = jnp.dot(a_ref[...], w_ref[...], preferred_element_type=F32)
        zq_ref[...] = (z * jnp.where(j == 0, HEAD_DIM ** -0.5, 1.0)).astype(zq_ref.dtype)
        zs[...] = z
        for it in range(n_blocks):
            conv_block(it)
        norm_rows()

    @pl.when((j < nj - 1) & (i == n_tiles))
    def _():
        def body(it, _):
            conv_block(it)
            return 0

        lax.fori_loop(0, n_blocks, body, 0)
        norm_rows()

    @pl.when((j == nj - 1) & (i < n_tiles))
    def _():
        gate = jnp.dot(a_ref[...], w_ref[...], preferred_element_type=F32)
        first = (i % tiles_per_seq) == 0
        ubuf[cur, pl.ds(0, HALO), :] = jnp.where(first, 0.0, ubuf[prev, pl.ds(tm, HALO), :])
        ubuf[cur, pl.ds(HALO, tm), :] = zs[...] * jax.nn.sigmoid(gate)
        for src, dst in zip(cast_src, cast_dst):
            dst[...] = src[...].astype(dst.dtype)


def _fused_in(x2, g, w, w_dw, b_dw, g_ln, b_ln, seq, cast_weights, tm=512):
    M, D = x2.shape
    N = w.shape[1]
    C = w_dw.shape[1]
    tn = D_ATTN
    nj = N // tn
    n_tiles = M // tm
    groups = tm // CONV_ROWS
    assert C == tn and N == nj * tn and groups % (nj - 1) == 0 and seq % tm == 0
    last = n_tiles - 1
    vec = pl.BlockSpec((1, C), lambda i, j: (0, 0))
    def slab(wt, arrive=0):
        return pl.BlockSpec((wt.shape[0] // n_tiles, wt.shape[1]),
                            lambda i, j: (jnp.clip(i - jnp.where(j < arrive, 1, 0), 0, last), 0))

    arrivals = [1 + k % (nj - 1) for k in range(len(cast_weights))]
    assert all(wt.shape[0] % (16 * n_tiles) == 0 for wt in cast_weights)
    return pl.pallas_call(
        functools.partial(_fused_in_kernel, n_tiles=n_tiles, tiles_per_seq=seq // tm,
                          groups_per_step=groups // (nj - 1), n_cast=len(cast_weights)),
        grid=(n_tiles + 1, nj),
        in_specs=[
            pl.BlockSpec((tm, D), lambda i, j: (jnp.minimum(i, last), 0)),
            pl.BlockSpec((1, D), lambda i, j: (0, 0)),
            pl.BlockSpec((D, tn), lambda i, j: (0, j)),
            pl.BlockSpec((CONV_WIDTH, C), lambda i, j: (0, 0)),
            vec, vec, vec,
        ] + [slab(wt, s) for wt, s in zip(cast_weights, arrivals)],
        out_specs=[
            pl.BlockSpec((tm, tn), lambda i, j: (jnp.minimum(i, last),
                                                 jnp.where(i > last, nj - 2, jnp.minimum(j, nj - 2)))),
            pl.BlockSpec((tm, C), lambda i, j: (jnp.maximum(i - 1, 0), 0)),
        ] + [slab(wt) for wt in cast_weights],
        out_shape=[jax.ShapeDtypeStruct((M, (nj - 1) * tn), BF16), jax.ShapeDtypeStruct((M, C), BF16)]
        + [jax.ShapeDtypeStruct(wt.shape, BF16) for wt in cast_weights],
        scratch_shapes=[
            pltpu.VMEM((tm, D), BF16),
            pltpu.VMEM((tm, C), F32),
            pltpu.VMEM((2, HALO + tm, C), F32),
            pltpu.VMEM((groups // (nj - 1), CONV_ROWS, C), F32),
        ],
        compiler_params=_cparams(("arbitrary", "arbitrary")),
        name="in_proj_conv",
    )(x2, g, w, w_dw, b_dw, g_ln, b_ln, *cast_weights)


def _out_proj_kernel(oa_ref, oc_ref, x_ref, wa_ref, wc_ref, g_ref, h_ref, f_ref):
    acc = jnp.dot(oa_ref[...], wa_ref[...], preferred_element_type=F32)
    acc = acc + jnp.dot(oc_ref[...], wc_ref[...], preferred_element_type=F32)
    h = x_ref[...] + acc
    h_ref[...] = h
    f_ref[...] = _rms(h, g_ref[...]).astype(f_ref.dtype)


def _out_proj(oa, oc, x2, w_out, g, tm=512):
    M, D = x2.shape
    Ka, Kc = oa.shape[1], oc.shape[1]
    row = lambda n: pl.BlockSpec((tm, n), lambda i: (i, 0))
    return pl.pallas_call(
        _out_proj_kernel,
        grid=(M // tm,),
        in_specs=[
            row(Ka), row(Kc), row(D),
            pl.BlockSpec((Ka, D), lambda i: (0, 0)),
            pl.BlockSpec((Kc, D), lambda i: (Ka // Kc, 0)),
            pl.BlockSpec((1, D), lambda i: (0, 0)),
        ],
        out_specs=[row(D), row(D)],
        out_shape=[jax.ShapeDtypeStruct((M, D), F32), jax.ShapeDtypeStruct((M, D), BF16)],
        compiler_params=_cparams(("parallel",)),
        name="out_proj",
    )(oa, oc, x2, w_out, w_out, g)


def _ffn_kernel(f_ref, wg_ref, wu_ref, wd_ref, o_ref):
    @pl.when(pl.program_id(1) == 0)
    def _():
        o_ref[...] = jnp.zeros_like(o_ref)

    f = f_ref[...]
    g = jnp.dot(f, wg_ref[...], preferred_element_type=F32)
    u = jnp.dot(f, wu_ref[...], preferred_element_type=F32)
    act = (g * jax.nn.sigmoid(g) * u).astype(BF16)
    o_ref[...] += jnp.dot(act, wd_ref[...], preferred_element_type=F32)


def _ffn(f, wg, wu, wd, tm=1024, tf=512):
    M, D = f.shape
    F = wg.shape[1]
    return pl.pallas_call(
        _ffn_kernel,
        grid=(M // tm, F // tf),
        in_specs=[
            pl.BlockSpec((tm, D), lambda i, j: (i, 0)),
            pl.BlockSpec((D, tf), lambda i, j: (0, j)),
            pl.BlockSpec((D, tf), lambda i, j: (0, j)),
            pl.BlockSpec((tf, D), lambda i, j: (j, 0)),
        ],
        out_specs=pl.BlockSpec((tm, D), lambda i, j: (i, 0)),
        out_shape=jax.ShapeDtypeStruct((M, D), F32),
        compiler_params=_cparams(("parallel", "arbitrary")),
        name="ffn",
    )(f, wg, wu, wd)


def _ple_kernel(h_ref, d_ref, p_ref, wpg_ref, bpg_ref, wple_ref, gp_ref, gf_ref, o_ref, *, final):
    h2 = h_ref[...] + d_ref[...]
    a = _rms(h2, gp_ref[...]).astype(BF16)
    gte = jax.nn.sigmoid(jnp.dot(a, wpg_ref[...], preferred_element_type=F32) + bpg_ref[...])
    ple = jnp.dot(p_ref[...].astype(BF16), wple_ref[...], preferred_element_type=F32)
    h3 = h2 + ple * gte
    o_ref[...] = _rms(h3, gf_ref[...]) if final else h3


def _ple(h1, d, p2, wpg, bpg, wple, g_ple, g_final, final, tm=512):
    M, D = h1.shape
    P = p2.shape[1]
    row = lambda n: pl.BlockSpec((tm, n), lambda i: (i, 0))
    full = lambda r, c: pl.BlockSpec((r, c), lambda i: (0, 0))
    return pl.pallas_call(
        functools.partial(_ple_kernel, final=final),
        grid=(M // tm,),
        in_specs=[row(D), row(D), row(P), full(D, D), full(1, D), full(P, D), full(1, D), full(1, D)],
        out_specs=row(D),
        out_shape=jax.ShapeDtypeStruct((M, D), F32),
        compiler_params=_cparams(("parallel",)),
        name="ple_final",
    )(h1, d, p2, wpg, bpg, wple, g_ple, g_final)


def kernel(x, p, g_mix, w_in, w_dw, b_dw, g_conv_ln, b_conv_ln, w_out, g_ffn, w_gate, w_up, w_down,
           g_ple, w_pgate, b_pgate, w_ple, g_final):
    B, S, D = x.shape
    depth = w_in.shape[0]
    M = B * S
    h = x.reshape(M, D)
    out = None
    for i in range(depth):
        last = i == depth - 1
        zq, oc, wo, wg, wu, wd, wpg = _fused_in(
            h, g_mix[i][None], w_in[i].astype(BF16), w_dw[i, :, 0, :], b_dw[i][None], g_conv_ln[i][None],
            b_conv_ln[i][None], S, (w_out[i], w_gate[i], w_up[i], w_down[i], w_pgate[i]))
        oa = _attention(zq.reshape(B, S, -1)).reshape(M, -1)
        h1, f = _out_proj(oa, oc, h, wo, g_ffn[i][None])
        d = _ffn(f, wg, wu, wd)
        out = _ple(h1, d, p[i].reshape(M, -1), wpg, b_pgate[i][None],
                   w_ple[i].astype(BF16), g_ple[i][None], g_final[None], last)
        h = out
    return out.reshape(B, S, D)
```

```python
import functools

import jax
import jax.numpy as jnp
from jax import lax
from jax.experimental import pallas as pl
from jax.experimental.pallas import tpu as pltpu

F32 = jnp.float32
BF16 = jnp.bfloat16

EPS = 1e-6
N_HEADS = 8
HEAD_DIM = 128
D_ATTN = N_HEADS * HEAD_DIM
BLK = 128
DILATIONS = (1, 4, 16)
CONV_WIDTH = 31
CHUNK = DILATIONS[-1] * BLK
NEG = -1e30
Q_SCALE = HEAD_DIM ** -0.5 * 1.4426950408889634

VMEM_LIMIT = 56 * 1024 * 1024


def _cparams(sem):
    return pltpu.CompilerParams(dimension_semantics=sem, vmem_limit_bytes=VMEM_LIMIT)


def _rms(x, g):
    return x * lax.rsqrt(jnp.mean(x * x, axis=-1, keepdims=True) + EPS) * g

NSLAB = 4
SLAB_CHUNK = CHUNK // NSLAB
PIECE = BLK // NSLAB


def _attn_kernel(q_ref, k_ref, v_ref, o_ref, qf, kf, vf, num, mm, ll, stage, onat, bias, *, seq):
    cp = NSLAB * BLK
    n_chunks = seq // CHUNK

    def cast_piece(i):
        t0 = pl.multiple_of(i * cp, cp)
        s0 = pl.multiple_of(i * BLK, BLK)
        for j, (src, dst) in enumerate(((q_ref, qf), (k_ref, kf), (v_ref, vf))):
            stage[j] = src[pl.ds(t0, cp), :].astype(F32)
            for r in range(NSLAB):
                dst[r, pl.ds(s0, BLK), :] = stage[j, pl.ds(r, BLK, stride=NSLAB), :]

    def cast_body(i, _):
        cast_piece(i)
        return 0

    lax.fori_loop(0, CHUNK // cp, cast_body, 0)

    qi = lax.broadcasted_iota(jnp.int32, (BLK, 2 * BLK), 0)
    kj = lax.broadcasted_iota(jnp.int32, (BLK, 2 * BLK), 1)

    def put(slot, dist):
        bias[slot] = jnp.where((dist >= 0) & (dist <= BLK), 0.0, NEG).astype(F32)

    put(0, qi + BLK - kj)
    put(1, qi - kj)
    d1 = NSLAB * (qi % PIECE - kj % (2 * PIECE)) + (qi // PIECE - kj // (2 * PIECE))
    put(2, d1 + BLK)
    put(3, d1)

    ones = jnp.ones((2 * BLK, HEAD_DIM), BF16)

    def block(q, k, v, bias_slot):
        s = lax.dot_general(q, k, (((1,), (1,)), ((), ())), preferred_element_type=F32)
        s = s + bias[bias_slot]
        m = jnp.max(s, axis=-1, keepdims=True)
        e = jnp.exp2(s - m).astype(BF16)
        pv = jnp.dot(e, jnp.concatenate([v, ones], axis=1), preferred_element_type=F32)
        return pv[:, :HEAD_DIM], jnp.broadcast_to(m, (BLK, HEAD_DIM)), pv[:, HEAD_DIM:]

    def chunk_body(c, _):
        r0 = pl.multiple_of(c * SLAB_CHUNK, SLAB_CHUNK)
        first_chunk = jnp.where(c == 0, 1, 0)

        nxt = jnp.minimum(c + 1, n_chunks - 1)
        for p in range(CHUNK // cp):
            cast_piece(nxt * (CHUNK // cp) + p)

        for mres in range(NSLAB):
            k0 = jnp.maximum(r0 - SLAB_CHUNK, 0) + mres
            for r in range(NSLAB):
                q = qf[r, pl.ds(r0 + mres, BLK, stride=NSLAB), :].astype(BF16)
                k = kf[r, pl.ds(k0, 2 * BLK, stride=NSLAB), :].astype(BF16)
                v = vf[r, pl.ds(k0, 2 * BLK, stride=NSLAB), :].astype(BF16)
                pv, m, l = block(q, k, v, first_chunk)
                dst = pl.ds(mres, BLK, stride=NSLAB)
                num[r, dst, :] = pv
                mm[r, dst, :] = m
                ll[r, dst, :] = l

        for nb in range(SLAB_CHUNK // BLK):
            loc = nb * BLK
            q0 = pl.multiple_of(r0 + loc, BLK)
            first = jnp.where(q0 == 0, 1, 0)
            k0 = pl.multiple_of(jnp.maximum(q0 - BLK, 0), BLK)
            for r in range(NSLAB):
                q = qf[r, pl.ds(q0, BLK), :].astype(BF16)
                k = kf[r, pl.ds(k0, 2 * BLK), :].astype(BF16)
                v = vf[r, pl.ds(k0, 2 * BLK), :].astype(BF16)
                pv, m, l = block(q, k, v, first)
                dst = pl.ds(loc, BLK)
                num[NSLAB + r, dst, :] = pv
                mm[NSLAB + r, dst, :] = m
                ll[NSLAB + r, dst, :] = l

        def combine(g):
            rows = pl.ds(g * BLK, BLK)
            for r in range(NSLAB):
                m0, m1, m2 = mm[r, rows, :], mm[NSLAB + r, rows, :], mm[2 * NSLAB + r, rows, :]
                mx = jnp.maximum(jnp.maximum(m0, m1), m2)
                a0, a1, a2 = jnp.exp2(m0 - mx), jnp.exp2(m1 - mx), jnp.exp2(m2 - mx)
                den = a0 * ll[r, rows, :] + a1 * ll[NSLAB + r, rows, :] + a2 * ll[2 * NSLAB + r, rows, :]
                o = (a0 * num[r, rows, :] + a1 * num[NSLAB + r, rows, :]
                     + a2 * num[2 * NSLAB + r, rows, :]) / den
                onat[g, pl.ds(r, BLK, stride=NSLAB), :] = o
            t0 = pl.multiple_of(c * CHUNK + g * (NSLAB * BLK), NSLAB * BLK)
            o_ref[pl.ds(t0, NSLAB * BLK), :] = onat[g].astype(o_ref.dtype)

        for nb in range(SLAB_CHUNK // PIECE):
            loc = nb * PIECE
            q0 = pl.multiple_of(r0 + loc, PIECE)
            first = jnp.where(q0 == 0, 1, 0)
            k0 = pl.multiple_of(jnp.maximum(q0 - PIECE, 0), PIECE)
            gather = lambda ref, start, n: jnp.concatenate(
                [ref[r, pl.ds(start, n), :] for r in range(NSLAB)], axis=0).astype(BF16)
            q = gather(qf, q0, PIECE)
            k = gather(kf, k0, 2 * PIECE)
            v = gather(vf, k0, 2 * PIECE)
            pv, m, l = block(q, k, v, 2 + first)
            for r in range(NSLAB):
                dst = pl.ds(loc, PIECE)
                src = slice(r * PIECE, (r + 1) * PIECE)
                num[2 * NSLAB + r, dst, :] = pv[src]
                mm[2 * NSLAB + r, dst, :] = m[src]
                ll[2 * NSLAB + r, dst, :] = l[src]
            if (loc + PIECE) % BLK == 0:
                combine(loc // BLK)
        return 0

    lax.fori_loop(0, n_chunks, chunk_body, 0)


def _attention(z3):
    B, S, _ = z3.shape
    rows4 = S // NSLAB
    blk = lambda off: pl.BlockSpec((None, S, HEAD_DIM), lambda b, h, off=off: (b, 0, off + h))
    return pl.pallas_call(
        functools.partial(_attn_kernel, seq=S),
        grid=(B, N_HEADS),
        in_specs=[blk(0), blk(N_HEADS), blk(2 * N_HEADS)],
        out_specs=pl.BlockSpec((None, S, HEAD_DIM), lambda b, h: (b, 0, h)),
        out_shape=jax.ShapeDtypeStruct((B, S, D_ATTN), BF16),
        scratch_shapes=[
            pltpu.VMEM((NSLAB, rows4, HEAD_DIM), F32),
            pltpu.VMEM((NSLAB, rows4, HEAD_DIM), F32),
            pltpu.VMEM((NSLAB, rows4, HEAD_DIM), F32),
            pltpu.VMEM((3 * NSLAB, SLAB_CHUNK, HEAD_DIM), F32),
            pltpu.VMEM((3 * NSLAB, SLAB_CHUNK, HEAD_DIM), F32),
            pltpu.VMEM((3 * NSLAB, SLAB_CHUNK, HEAD_DIM), F32),
            pltpu.VMEM((3, NSLAB * BLK, HEAD_DIM), F32),
            pltpu.VMEM((SLAB_CHUNK // BLK, NSLAB * BLK, HEAD_DIM), F32),
            pltpu.VMEM((4, BLK, 2 * BLK), F32),
        ],
        compiler_params=_cparams(("parallel", "parallel")),
        name="dilated_attn",
    )(z3, z3, z3)


SUBLANES = 8
LANES = 128
HALO = 32
CONV_ROWS = 64
NORM_ROWS = 16


def _fused_in_kernel(*refs, n_tiles, tiles_per_seq, groups_per_step, n_cast):
    x_ref, g_ref, w_ref, wdw_ref, bdw_ref, gln_ref, bln_ref = refs[:7]
    cast_src = refs[7:7 + n_cast]
    zq_ref, oc_ref = refs[7 + n_cast:9 + n_cast]
    cast_dst = refs[9 + n_cast:9 + 2 * n_cast]
    a_ref, zs, ubuf, yrow = refs[9 + 2 * n_cast:]
    i, j = pl.program_id(0), pl.program_id(1)
    nj = pl.num_programs(1)
    tm, C = zs.shape
    cur, prev = i % 2, (i + 1) % 2
    off = HALO - (CONV_WIDTH - 1)
    win_rows = CONV_ROWS + HALO

    @pl.when((i == 0) & (j == 0))
    def _():
        ubuf[...] = jnp.zeros_like(ubuf)

    @pl.when(j == 0)
    def _():
        a_ref[...] = _rms(x_ref[...], g_ref[...]).astype(BF16)

    n_blocks = groups_per_step * (C // LANES)

    def conv_block(it):
        k = it // (C // LANES)
        c0 = (it % (C // LANES)) * LANES
        if not isinstance(it, int):
            c0 = pl.multiple_of(c0, LANES)
        t0 = pl.multiple_of((j * groups_per_step + k) * CONV_ROWS, CONV_ROWS)
        win = ubuf[prev, pl.ds(t0, win_rows), pl.ds(c0, LANES)]
        acc = jnp.zeros((CONV_ROWS, LANES), F32)
        for s in range(SUBLANES):
            ys = win if s == 0 else pltpu.roll(win, win_rows - s, axis=0)
            for blk8 in range(HALO // SUBLANES + 1):
                tap = SUBLANES * blk8 + s - off
                if 0 <= tap < CONV_WIDTH:
                    acc = acc + (ys[SUBLANES * blk8:SUBLANES * blk8 + CONV_ROWS]
                                 * wdw_ref[pl.ds(tap, 1), pl.ds(c0, LANES)])
        yrow[k, :, pl.ds(c0, LANES)] = acc

    def norm_rows():
        for k in range(groups_per_step):
            t0 = pl.multiple_of((j * groups_per_step + k) * CONV_ROWS, CONV_ROWS)
            for r in range(0, CONV_ROWS, NORM_ROWS):
                y = yrow[k, r:r + NORM_ROWS, :] + bdw_ref[...]
                mu = jnp.mean(y, axis=-1, keepdims=True)
                yc = y - mu
                var = jnp.mean(yc * yc, axis=-1, keepdims=True)
                yn = yc * lax.rsqrt(var + EPS) * gln_ref[...] + bln_ref[...]
                oc_ref[pl.ds(t0 + r, NORM_ROWS), :] = (yn * jax.nn.sigmoid(yn)).astype(oc_ref.dtype)

    @pl.when((j < nj - 1) & (i < n_tiles))
    def _():
        z = jnp.dot(a_ref[...], w_ref[...], preferred_element_type=F32)
        zq_ref[...] = (z * jnp.where(j == 0, Q_SCALE, 1.0)).astype(zq_ref.dtype)
        zs[...] = z
        for it in range(n_blocks):
            conv_block(it)
        norm_rows()

    @pl.when((j < nj - 1) & (i == n_tiles))
    def _():
        def body(it, _):
            conv_block(it)
            return 0

        lax.fori_loop(0, n_blocks, body, 0)
        norm_rows()

    @pl.when((j == nj - 1) & (i < n_tiles))
    def _():
        gate = jnp.dot(a_ref[...], w_ref[...], preferred_element_type=F32)
        first = (i % tiles_per_seq) == 0
        ubuf[cur, pl.ds(0, HALO), :] = jnp.where(first, 0.0, ubuf[prev, pl.ds(tm, HALO), :])
        ubuf[cur, pl.ds(HALO, tm), :] = zs[...] * jax.nn.sigmoid(gate)
        for src, dst in zip(cast_src, cast_dst):
            dst[...] = src[...].astype(dst.dtype)


def _fused_in(x2, g, w, w_dw, b_dw, g_ln, b_ln, seq, cast_weights, tm=512):
    M, D = x2.shape
    N = w.shape[1]
    C = w_dw.shape[1]
    tn = D_ATTN
    nj = N // tn
    n_tiles = M // tm
    groups = tm // CONV_ROWS
    assert C == tn and N == nj * tn and groups % (nj - 1) == 0 and seq % tm == 0
    last = n_tiles - 1
    vec = pl.BlockSpec((1, C), lambda i, j: (0, 0))
    def slab(wt, arrive=0):
        return pl.BlockSpec((wt.shape[0] // n_tiles, wt.shape[1]),
                            lambda i, j: (jnp.clip(i - jnp.where(j < arrive, 1, 0), 0, last), 0))

    arrivals = [1 + k % (nj - 1) for k in range(len(cast_weights))]
    assert all(wt.shape[0] % (16 * n_tiles) == 0 for wt in cast_weights)
    return pl.pallas_call(
        functools.partial(_fused_in_kernel, n_tiles=n_tiles, tiles_per_seq=seq // tm,
                          groups_per_step=groups // (nj - 1), n_cast=len(cast_weights)),
        grid=(n_tiles + 1, nj),
        in_specs=[
            pl.BlockSpec((tm, D), lambda i, j: (jnp.minimum(i, last), 0)),
            pl.BlockSpec((1, D), lambda i, j: (0, 0)),
            pl.BlockSpec((D, tn), lambda i, j: (0, j)),
            pl.BlockSpec((CONV_WIDTH, C), lambda i, j: (0, 0)),
            vec, vec, vec,
        ] + [slab(wt, s) for wt, s in zip(cast_weights, arrivals)],
        out_specs=[
            pl.BlockSpec((tm, tn), lambda i, j: (jnp.minimum(i, last),
                                                 jnp.where(i > last, nj - 2, jnp.minimum(j, nj - 2)))),
            pl.BlockSpec((tm, C), lambda i, j: (jnp.maximum(i - 1, 0), 0)),
        ] + [slab(wt) for wt in cast_weights],
        out_shape=[jax.ShapeDtypeStruct((M, (nj - 1) * tn), BF16), jax.ShapeDtypeStruct((M, C), BF16)]
        + [jax.ShapeDtypeStruct(wt.shape, BF16) for wt in cast_weights],
        scratch_shapes=[
            pltpu.VMEM((tm, D), BF16),
            pltpu.VMEM((tm, C), F32),
            pltpu.VMEM((2, HALO + tm, C), F32),
            pltpu.VMEM((groups // (nj - 1), CONV_ROWS, C), F32),
        ],
        compiler_params=_cparams(("arbitrary", "arbitrary")),
        name="in_proj_conv",
    )(x2, g, w, w_dw, b_dw, g_ln, b_ln, *cast_weights)


def _out_proj_kernel(oa_ref, oc_ref, x_ref, wa_ref, wc_ref, g_ref, h_ref, f_ref):
    acc = jnp.dot(oa_ref[...], wa_ref[...], preferred_element_type=F32)
    acc = acc + jnp.dot(oc_ref[...], wc_ref[...], preferred_element_type=F32)
    h = x_ref[...] + acc
    h_ref[...] = h
    f_ref[...] = _rms(h, g_ref[...]).astype(f_ref.dtype)


def _out_proj(oa, oc, x2, w_out, g, tm=512):
    M, D = x2.shape
    Ka, Kc = oa.shape[1], oc.shape[1]
    row = lambda n: pl.BlockSpec((tm, n), lambda i: (i, 0))
    return pl.pallas_call(
        _out_proj_kernel,
        grid=(M // tm,),
        in_specs=[
            row(Ka), row(Kc), row(D),
            pl.BlockSpec((Ka, D), lambda i: (0, 0)),
            pl.BlockSpec((Kc, D), lambda i: (Ka // Kc, 0)),
            pl.BlockSpec((1, D), lambda i: (0, 0)),
        ],
        out_specs=[row(D), row(D)],
        out_shape=[jax.ShapeDtypeStruct((M, D), F32), jax.ShapeDtypeStruct((M, D), BF16)],
        compiler_params=_cparams(("parallel",)),
        name="out_proj",
    )(oa, oc, x2, w_out, w_out, g)


def _ffn_kernel(f_ref, wg_ref, wu_ref, wd_ref, o_ref):
    @pl.when(pl.program_id(1) == 0)
    def _():
        o_ref[...] = jnp.zeros_like(o_ref)

    f = f_ref[...]
    g = jnp.dot(f, wg_ref[...], preferred_element_type=F32)
    u = jnp.dot(f, wu_ref[...], preferred_element_type=F32)
    act = (g * jax.nn.sigmoid(g) * u).astype(BF16)
    o_ref[...] += jnp.dot(act, wd_ref[...], preferred_element_type=F32)


def _ffn(f, wg, wu, wd, tm=1024, tf=512):
    M, D = f.shape
    F = wg.shape[1]
    return pl.pallas_call(
        _ffn_kernel,
        grid=(M // tm, F // tf),
        in_specs=[
            pl.BlockSpec((tm, D), lambda i, j: (i, 0)),
            pl.BlockSpec((D, tf), lambda i, j: (0, j)),
            pl.BlockSpec((D, tf), lambda i, j: (0, j)),
            pl.BlockSpec((tf, D), lambda i, j: (j, 0)),
        ],
        out_specs=pl.BlockSpec((tm, D), lambda i, j: (i, 0)),
        out_shape=jax.ShapeDtypeStruct((M, D), F32),
        compiler_params=_cparams(("parallel", "arbitrary")),
        name="ffn",
    )(f, wg, wu, wd)


def _ple_kernel(h_ref, d_ref, p_ref, wpg_ref, bpg_ref, wple_ref, gp_ref, gf_ref, o_ref, *, final):
    h2 = h_ref[...] + d_ref[...]
    a = _rms(h2, gp_ref[...]).astype(BF16)
    gte = jax.nn.sigmoid(jnp.dot(a, wpg_ref[...], preferred_element_type=F32) + bpg_ref[...])
    ple = jnp.dot(p_ref[...].astype(BF16), wple_ref[...], preferred_element_type=F32)
    h3 = h2 + ple * gte
    o_ref[...] = _rms(h3, gf_ref[...]) if final else h3


def _ple(h1, d, p2, wpg, bpg, wple, g_ple, g_final, final, tm=512):
    M, D = h1.shape
    P = p2.shape[1]
    row = lambda n: pl.BlockSpec((tm, n), lambda i: (i, 0))
    full = lambda r, c: pl.BlockSpec((r, c), lambda i: (0, 0))
    return pl.pallas_call(
        functools.partial(_ple_kernel, final=final),
        grid=(M // tm,),
        in_specs=[row(D), row(D), row(P), full(D, D), full(1, D), full(P, D), full(1, D), full(1, D)],
        out_specs=row(D),
        out_shape=jax.ShapeDtypeStruct((M, D), F32),
        compiler_params=_cparams(("parallel",)),
        name="ple_final",
    )(h1, d, p2, wpg, bpg, wple, g_ple, g_final)


def kernel(x, p, g_mix, w_in, w_dw, b_dw, g_conv_ln, b_conv_ln, w_out, g_ffn, w_gate, w_up, w_down,
           g_ple, w_pgate, b_pgate, w_ple, g_final):
    B, S, D = x.shape
    depth = w_in.shape[0]
    M = B * S
    h = x.reshape(M, D)
    out = None
    for i in range(depth):
        last = i == depth - 1
        zq, oc, wo, wg, wu, wd, wpg = _fused_in(
            h, g_mix[i][None], w_in[i].astype(BF16), w_dw[i, :, 0, :], b_dw[i][None], g_conv_ln[i][None],
            b_conv_ln[i][None], S, (w_out[i], w_gate[i], w_up[i], w_down[i], w_pgate[i]))
        oa = _attention(zq.reshape(B, S, -1)).reshape(M, -1)
        h1, f = _out_proj(oa, oc, h, wo, g_ffn[i][None])
        d = _ffn(f, wg, wu, wd)
        out = _ple(h1, d, p[i].reshape(M, -1), wpg, b_pgate[i][None],
                   w_ple[i].astype(BF16), g_ple[i][None], g_final[None], last)
        h = out
    return out.reshape(B, S, D)
```

```python
import functools

import jax
import jax.numpy as jnp
from jax import lax
from jax.experimental import pallas as pl
from jax.experimental.pallas import tpu as pltpu

F32 = jnp.float32
BF16 = jnp.bfloat16

EPS = 1e-6
N_HEADS = 8
HEAD_DIM = 128
D_ATTN = N_HEADS * HEAD_DIM
BLK = 128
DILATIONS = (1, 4, 16)
CONV_WIDTH = 31
CHUNK = DILATIONS[-1] * BLK
NEG = -1e30
Q_SCALE = HEAD_DIM ** -0.5 * 1.4426950408889634

VMEM_LIMIT = 56 * 1024 * 1024


def _cparams(sem):
    return pltpu.CompilerParams(dimension_semantics=sem, vmem_limit_bytes=VMEM_LIMIT)


def _rms(x, g):
    return x * lax.rsqrt(jnp.mean(x * x, axis=-1, keepdims=True) + EPS) * g

NSLAB = 4
SLAB_CHUNK = CHUNK // NSLAB
PIECE = BLK // NSLAB


def _attn_kernel(q_ref, k_ref, v_ref, o_ref, qf, kf, vf, num, mm, ll, stage, onat, bias, *, seq):
    cp = NSLAB * BLK
    n_chunks = seq // CHUNK

    def cast_piece(i):
        t0 = pl.multiple_of(i * cp, cp)
        s0 = pl.multiple_of(i * BLK, BLK)
        for j, (src, dst) in enumerate(((q_ref, qf), (k_ref, kf), (v_ref, vf))):
            stage[j] = src[pl.ds(t0, cp), :].astype(F32)
            for r in range(NSLAB):
                dst[r, pl.ds(s0, BLK), :] = stage[j, pl.ds(r, BLK, stride=NSLAB), :]

    def cast_body(i, _):
        cast_piece(i)
        return 0

    lax.fori_loop(0, CHUNK // cp, cast_body, 0)

    qi = lax.broadcasted_iota(jnp.int32, (BLK, 2 * BLK), 0)
    kj = lax.broadcasted_iota(jnp.int32, (BLK, 2 * BLK), 1)

    def put(slot, dist):
        bias[slot] = jnp.where((dist >= 0) & (dist <= BLK), 0.0, NEG).astype(F32)

    put(0, qi + BLK - kj)
    put(1, qi - kj)
    d1 = NSLAB * (qi % PIECE - kj % (2 * PIECE)) + (qi // PIECE - kj // (2 * PIECE))
    put(2, d1 + BLK)
    put(3, d1)

    ones = jnp.ones((2 * BLK, HEAD_DIM), BF16)

    def block(q, k, v, bias_slot):
        s = lax.dot_general(q, k, (((1,), (1,)), ((), ())), preferred_element_type=F32)
        s = s + bias[bias_slot]
        m = jnp.max(s, axis=-1, keepdims=True)
        e = jnp.exp2(s - m).astype(BF16)
        pv = jnp.dot(e, jnp.concatenate([v, ones], axis=1), preferred_element_type=F32)
        return pv[:, :HEAD_DIM], jnp.broadcast_to(m, (BLK, HEAD_DIM)), pv[:, HEAD_DIM:]

    def chunk_body(c, _):
        r0 = pl.multiple_of(c * SLAB_CHUNK, SLAB_CHUNK)
        first_chunk = jnp.where(c == 0, 1, 0)

        nxt = jnp.minimum(c + 1, n_chunks - 1)
        for p in range(CHUNK // cp):
            cast_piece(nxt * (CHUNK // cp) + p)

        for mres in range(NSLAB):
            k0 = jnp.maximum(r0 - SLAB_CHUNK, 0) + mres
            for r in range(NSLAB):
                q = qf[r, pl.ds(r0 + mres, BLK, stride=NSLAB), :].astype(BF16)
                k = kf[r, pl.ds(k0, 2 * BLK, stride=NSLAB), :].astype(BF16)
                v = vf[r, pl.ds(k0, 2 * BLK, stride=NSLAB), :].astype(BF16)
                pv, m, l = block(q, k, v, first_chunk)
                dst = pl.ds(mres, BLK, stride=NSLAB)
                num[r, dst, :] = pv
                mm[r, dst, :] = m
                ll[r, dst, :] = l

        for nb in range(SLAB_CHUNK // BLK):
            loc = nb * BLK
            q0 = pl.multiple_of(r0 + loc, BLK)
            first = jnp.where(q0 == 0, 1, 0)
            k0 = pl.multiple_of(jnp.maximum(q0 - BLK, 0), BLK)
            for r in range(NSLAB):
                q = qf[r, pl.ds(q0, BLK), :].astype(BF16)
                k = kf[r, pl.ds(k0, 2 * BLK), :].astype(BF16)
                v = vf[r, pl.ds(k0, 2 * BLK), :].astype(BF16)
                pv, m, l = block(q, k, v, first)
                dst = pl.ds(loc, BLK)
                num[NSLAB + r, dst, :] = pv
                mm[NSLAB + r, dst, :] = m
                ll[NSLAB + r, dst, :] = l

        def combine(g):
            rows = pl.ds(g * BLK, BLK)
            for r in range(NSLAB):
                m0, m1, m2 = mm[r, rows, :], mm[NSLAB + r, rows, :], mm[2 * NSLAB + r, rows, :]
                mx = jnp.maximum(jnp.maximum(m0, m1), m2)
                a0, a1, a2 = jnp.exp2(m0 - mx), jnp.exp2(m1 - mx), jnp.exp2(m2 - mx)
                den = a0 * ll[r, rows, :] + a1 * ll[NSLAB + r, rows, :] + a2 * ll[2 * NSLAB + r, rows, :]
                o = (a0 * num[r, rows, :] + a1 * num[NSLAB + r, rows, :]
                     + a2 * num[2 * NSLAB + r, rows, :]) / den
                onat[g, pl.ds(r, BLK, stride=NSLAB), :] = o
            t0 = pl.multiple_of(c * CHUNK + g * (NSLAB * BLK), NSLAB * BLK)
            o_ref[pl.ds(t0, NSLAB * BLK), :] = onat[g].astype(o_ref.dtype)

        for nb in range(SLAB_CHUNK // PIECE):
            loc = nb * PIECE
            q0 = pl.multiple_of(r0 + loc, PIECE)
            first = jnp.where(q0 == 0, 1, 0)
            k0 = pl.multiple_of(jnp.maximum(q0 - PIECE, 0), PIECE)
            gather = lambda ref, start, n: jnp.concatenate(
                [ref[r, pl.ds(start, n), :] for r in range(NSLAB)], axis=0).astype(BF16)
            q = gather(qf, q0, PIECE)
            k = gather(kf, k0, 2 * PIECE)
            v = gather(vf, k0, 2 * PIECE)
            pv, m, l = block(q, k, v, 2 + first)
            for r in range(NSLAB):
                dst = pl.ds(loc, PIECE)
                src = slice(r * PIECE, (r + 1) * PIECE)
                num[2 * NSLAB + r, dst, :] = pv[src]
                mm[2 * NSLAB + r, dst, :] = m[src]
                ll[2 * NSLAB + r, dst, :] = l[src]
            if (loc + PIECE) % BLK == 0:
                combine(loc // BLK)
        return 0

    lax.fori_loop(0, n_chunks, chunk_body, 0)


def _attention(z3):
    B, S, _ = z3.shape
    rows4 = S // NSLAB
    blk = lambda off: pl.BlockSpec((None, S, HEAD_DIM), lambda b, h, off=off: (b, 0, off + h))
    return pl.pallas_call(
        functools.partial(_attn_kernel, seq=S),
        grid=(B, N_HEADS),
        in_specs=[blk(0), blk(N_HEADS), blk(2 * N_HEADS)],
        out_specs=pl.BlockSpec((None, S, HEAD_DIM), lambda b, h: (b, 0, h)),
        out_shape=jax.ShapeDtypeStruct((B, S, D_ATTN), BF16),
        scratch_shapes=[
            pltpu.VMEM((NSLAB, rows4, HEAD_DIM), F32),
            pltpu.VMEM((NSLAB, rows4, HEAD_DIM), F32),
            pltpu.VMEM((NSLAB, rows4, HEAD_DIM), F32),
            pltpu.VMEM((3 * NSLAB, SLAB_CHUNK, HEAD_DIM), F32),
            pltpu.VMEM((3 * NSLAB, SLAB_CHUNK, HEAD_DIM), F32),
            pltpu.VMEM((3 * NSLAB, SLAB_CHUNK, HEAD_DIM), F32),
            pltpu.VMEM((3, NSLAB * BLK, HEAD_DIM), F32),
            pltpu.VMEM((SLAB_CHUNK // BLK, NSLAB * BLK, HEAD_DIM), F32),
            pltpu.VMEM((4, BLK, 2 * BLK), F32),
        ],
        compiler_params=_cparams(("parallel", "parallel")),
        name="dilated_attn",
    )(z3, z3, z3)


SUBLANES = 8
LANES = 128
HALO = 32
CONV_ROWS = 64
PIECE_COLS = 256
NORM_ROWS = 16


def _fused_in_kernel(*refs, n_tiles, tiles_per_seq, groups_per_step, n_cast):
    x_ref, g_ref, w_ref, wdw_ref, bdw_ref, gln_ref, bln_ref, zeros_ref = refs[:8]
    cast_src = refs[8:8 + n_cast]
    zq_ref, oc_ref = refs[8 + n_cast:10 + n_cast]
    cast_dst = refs[10 + n_cast:10 + 2 * n_cast]
    a_ref, zs, ubuf, yrow = refs[10 + 2 * n_cast:]
    i, j = pl.program_id(0), pl.program_id(1)
    nj = pl.num_programs(1)
    tm, C = zs.shape
    cur, prev = i % 2, (i + 1) % 2
    off = HALO - (CONV_WIDTH - 1)
    win_rows = CONV_ROWS + HALO

    @pl.when((i == 0) & (j == 0))
    def _():
        ubuf[...] = jnp.zeros_like(ubuf)

    @pl.when(j == 0)
    def _():
        a_ref[...] = _rms(x_ref[...], g_ref[...]).astype(BF16)

    n_blocks = groups_per_step * (C // LANES)

    def conv_block(it, start=None):
        k = it // (C // LANES)
        c0 = (it % (C // LANES)) * LANES
        if not isinstance(it, int):
            c0 = pl.multiple_of(c0, LANES)
        t0 = pl.multiple_of((j * groups_per_step + k) * CONV_ROWS, CONV_ROWS)
        win = ubuf[prev, pl.ds(t0, win_rows), pl.ds(c0, LANES)]
        acc = (jnp.zeros((CONV_ROWS, LANES), F32) if start is None
               else jnp.tile(start, (CONV_ROWS // SUBLANES, 1)))
        for s in range(SUBLANES):
            ys = win if s == 0 else pltpu.roll(win, win_rows - s, axis=0)
            for blk8 in range(HALO // SUBLANES + 1):
                tap = SUBLANES * blk8 + s - off
                if 0 <= tap < CONV_WIDTH:
                    acc = acc + (ys[SUBLANES * blk8:SUBLANES * blk8 + CONV_ROWS]
                                 * wdw_ref[pl.ds(tap, 1), pl.ds(c0, LANES)])
        yrow[k, :, pl.ds(c0, LANES)] = acc
        return acc

    def zero_of(value):
        return (pltpu.bitcast(value[:SUBLANES, :LANES], jnp.uint32) & zeros_ref[...]).astype(F32)

    def norm_rows():
        for k in range(groups_per_step):
            t0 = pl.multiple_of((j * groups_per_step + k) * CONV_ROWS, CONV_ROWS)
            for r in range(0, CONV_ROWS, NORM_ROWS):
                y = yrow[k, r:r + NORM_ROWS, :] + bdw_ref[...]
                mu = jnp.mean(y, axis=-1, keepdims=True)
                yc = y - mu
                var = jnp.mean(yc * yc, axis=-1, keepdims=True)
                yn = yc * lax.rsqrt(var + EPS) * gln_ref[...] + bln_ref[...]
                oc_ref[pl.ds(t0 + r, NORM_ROWS), :] = (yn * jax.nn.sigmoid(yn)).astype(oc_ref.dtype)

    @pl.when((j < nj - 1) & (i < n_tiles))
    def _():
        scale = jnp.where(j == 0, Q_SCALE, 1.0)
        lead = 2 * SUBLANES
        n_half = C // 2
        for it in range(n_blocks):
            zero = jnp.tile(zero_of(conv_block(it)).astype(BF16), (lead // SUBLANES, PIECE_COLS // LANES))
            k0, n0 = (it // 2) * PIECE_COLS, (it % 2) * n_half
            lhs = jnp.concatenate([a_ref[:lead, k0:k0 + PIECE_COLS] + zero, a_ref[lead:, k0:k0 + PIECE_COLS]],
                                  axis=0)
            part = jnp.dot(lhs, w_ref[k0:k0 + PIECE_COLS, n0:n0 + n_half], preferred_element_type=F32)
            zs[:, n0:n0 + n_half] = part if it < 2 else zs[:, n0:n0 + n_half] + part
        zq_ref[...] = (zs[...] * scale).astype(zq_ref.dtype)
        norm_rows()

    @pl.when((j < nj - 1) & (i == n_tiles))
    def _():
        def body(it, _):
            conv_block(it)
            return 0

        lax.fori_loop(0, n_blocks, body, 0)
        norm_rows()

    @pl.when((j == nj - 1) & (i < n_tiles))
    def _():
        gate = jnp.dot(a_ref[...], w_ref[...], preferred_element_type=F32)
        first = (i % tiles_per_seq) == 0
        ubuf[cur, pl.ds(0, HALO), :] = jnp.where(first, 0.0, ubuf[prev, pl.ds(tm, HALO), :])
        ubuf[cur, pl.ds(HALO, tm), :] = zs[...] * jax.nn.sigmoid(gate)
        for src, dst in zip(cast_src, cast_dst):
            dst[...] = src[...].astype(dst.dtype)


def _fused_in(x2, g, w, w_dw, b_dw, g_ln, b_ln, seq, cast_weights, tm=512):
    M, D = x2.shape
    N = w.shape[1]
    C = w_dw.shape[1]
    tn = D_ATTN
    nj = N // tn
    n_tiles = M // tm
    groups = tm // CONV_ROWS
    assert C == tn and N == nj * tn and groups % (nj - 1) == 0 and seq % tm == 0
    last = n_tiles - 1
    vec = pl.BlockSpec((1, C), lambda i, j: (0, 0))
    def slab(wt, arrive=0):
        return pl.BlockSpec((wt.shape[0] // n_tiles, wt.shape[1]),
                            lambda i, j: (jnp.clip(i - jnp.where(j < arrive, 1, 0), 0, last), 0))

    arrivals = [1 + k % (nj - 1) for k in range(len(cast_weights))]
    assert all(wt.shape[0] % (16 * n_tiles) == 0 for wt in cast_weights)
    return pl.pallas_call(
        functools.partial(_fused_in_kernel, n_tiles=n_tiles, tiles_per_seq=seq // tm,
                          groups_per_step=groups // (nj - 1), n_cast=len(cast_weights)),
        grid=(n_tiles + 1, nj),
        in_specs=[
            pl.BlockSpec((tm, D), lambda i, j: (jnp.minimum(i, last), 0)),
            pl.BlockSpec((1, D), lambda i, j: (0, 0)),
            pl.BlockSpec((D, tn), lambda i, j: (0, j)),
            pl.BlockSpec((CONV_WIDTH, C), lambda i, j: (0, 0)),
            vec, vec, vec,
            pl.BlockSpec((SUBLANES, LANES), lambda i, j: (0, 0)),
        ] + [slab(wt, s) for wt, s in zip(cast_weights, arrivals)],
        out_specs=[
            pl.BlockSpec((tm, tn), lambda i, j: (jnp.minimum(i, last),
                                                 jnp.where(i > last, nj - 2, jnp.minimum(j, nj - 2)))),
            pl.BlockSpec((tm, C), lambda i, j: (jnp.maximum(i - 1, 0), 0)),
        ] + [slab(wt) for wt in cast_weights],
        out_shape=[jax.ShapeDtypeStruct((M, (nj - 1) * tn), BF16), jax.ShapeDtypeStruct((M, C), BF16)]
        + [jax.ShapeDtypeStruct(wt.shape, BF16) for wt in cast_weights],
        scratch_shapes=[
            pltpu.VMEM((tm, D), BF16),
            pltpu.VMEM((tm, C), F32),
            pltpu.VMEM((2, HALO + tm, C), F32),
            pltpu.VMEM((groups // (nj - 1), CONV_ROWS, C), F32),
        ],
        compiler_params=_cparams(("arbitrary", "arbitrary")),
        name="in_proj_conv",
    )(x2, g, w, w_dw, b_dw, g_ln, b_ln, jnp.zeros((SUBLANES, LANES), jnp.uint32), *cast_weights)


def _out_proj_kernel(oa_ref, oc_ref, x_ref, wa_ref, wc_ref, g_ref, h_ref, f_ref):
    acc = jnp.dot(oa_ref[...], wa_ref[...], preferred_element_type=F32)
    acc = acc + jnp.dot(oc_ref[...], wc_ref[...], preferred_element_type=F32)
    h = x_ref[...] + acc
    h_ref[...] = h
    f_ref[...] = _rms(h, g_ref[...]).astype(f_ref.dtype)


def _out_proj(oa, oc, x2, w_out, g, tm=512):
    M, D = x2.shape
    Ka, Kc = oa.shape[1], oc.shape[1]
    row = lambda n: pl.BlockSpec((tm, n), lambda i: (i, 0))
    return pl.pallas_call(
        _out_proj_kernel,
        grid=(M // tm,),
        in_specs=[
            row(Ka), row(Kc), row(D),
            pl.BlockSpec((Ka, D), lambda i: (0, 0)),
            pl.BlockSpec((Kc, D), lambda i: (Ka // Kc, 0)),
            pl.BlockSpec((1, D), lambda i: (0, 0)),
        ],
        out_specs=[row(D), row(D)],
        out_shape=[jax.ShapeDtypeStruct((M, D), F32), jax.ShapeDtypeStruct((M, D), BF16)],
        compiler_params=_cparams(("parallel",)),
        name="out_proj",
    )(oa, oc, x2, w_out, w_out, g)


def _ffn_kernel(f_ref, wg_ref, wu_ref, wd_ref, o_ref):
    @pl.when(pl.program_id(1) == 0)
    def _():
        o_ref[...] = jnp.zeros_like(o_ref)

    f = f_ref[...]
    g = jnp.dot(f, wg_ref[...], preferred_element_type=F32)
    u = jnp.dot(f, wu_ref[...], preferred_element_type=F32)
    act = (g * jax.nn.sigmoid(g) * u).astype(BF16)
    o_ref[...] += jnp.dot(act, wd_ref[...], preferred_element_type=F32)


def _ffn(f, wg, wu, wd, tm=1024, tf=512):
    M, D = f.shape
    F = wg.shape[1]
    return pl.pallas_call(
        _ffn_kernel,
        grid=(M // tm, F // tf),
        in_specs=[
            pl.BlockSpec((tm, D), lambda i, j: (i, 0)),
            pl.BlockSpec((D, tf), lambda i, j: (0, j)),
            pl.BlockSpec((D, tf), lambda i, j: (0, j)),
            pl.BlockSpec((tf, D), lambda i, j: (j, 0)),
        ],
        out_specs=pl.BlockSpec((tm, D), lambda i, j: (i, 0)),
        out_shape=jax.ShapeDtypeStruct((M, D), F32),
        compiler_params=_cparams(("parallel", "arbitrary")),
        name="ffn",
    )(f, wg, wu, wd)


def _ple_kernel(h_ref, d_ref, p_ref, wpg_ref, bpg_ref, wple_ref, gp_ref, gf_ref, o_ref, *, final):
    h2 = h_ref[...] + d_ref[...]
    a = _rms(h2, gp_ref[...]).astype(BF16)
    gte = jax.nn.sigmoid(jnp.dot(a, wpg_ref[...], preferred_element_type=F32) + bpg_ref[...])
    ple = jnp.dot(p_ref[...].astype(BF16), wple_ref[...], preferred_element_type=F32)
    h3 = h2 + ple * gte
    o_ref[...] = _rms(h3, gf_ref[...]) if final else h3


def _ple(h1, d, p2, wpg, bpg, wple, g_ple, g_final, final, tm=512):
    M, D = h1.shape
    P = p2.shape[1]
    row = lambda n: pl.BlockSpec((tm, n), lambda i: (i, 0))
    full = lambda r, c: pl.BlockSpec((r, c), lambda i: (0, 0))
    return pl.pallas_call(
        functools.partial(_ple_kernel, final=final),
        grid=(M // tm,),
        in_specs=[row(D), row(D), row(P), full(D, D), full(1, D), full(P, D), full(1, D), full(1, D)],
        out_specs=row(D),
        out_shape=jax.ShapeDtypeStruct((M, D), F32),
        compiler_params=_cparams(("parallel",)),
        name="ple_final",
    )(h1, d, p2, wpg, bpg, wple, g_ple, g_final)


def kernel(x, p, g_mix, w_in, w_dw, b_dw, g_conv_ln, b_conv_ln, w_out, g_ffn, w_gate, w_up, w_down,
           g_ple, w_pgate, b_pgate, w_ple, g_final):
    B, S, D = x.shape
    depth = w_in.shape[0]
    M = B * S
    h = x.reshape(M, D)
    out = None
    for i in range(depth):
        last = i == depth - 1
        zq, oc, wo, wg, wu, wd, wpg = _fused_in(
            h, g_mix[i][None], w_in[i].astype(BF16), w_dw[i, :, 0, :], b_dw[i][None], g_conv_ln[i][None],
            b_conv_ln[i][None], S, (w_out[i], w_gate[i], w_up[i], w_down[i], w_pgate[i]))
        oa = _attention(zq.reshape(B, S, -1)).reshape(M, -1)
        h1, f = _out_proj(oa, oc, h, wo, g_ffn[i][None])
        d = _ffn(f, wg, wu, wd)
        out = _ple(h1, d, p[i].reshape(M, -1), wpg, b_pgate[i][None],
                   w_ple[i].astype(BF16), g_ple[i][None], g_final[None], last)
        h = out
    return out.reshape(B, S, D)
```

```python
import functools

import jax
import jax.numpy as jnp
from jax import lax
from jax.experimental import pallas as pl
from jax.experimental.pallas import tpu as pltpu

F32 = jnp.float32
BF16 = jnp.bfloat16

EPS = 1e-6
N_HEADS = 8
HEAD_DIM = 128
D_ATTN = N_HEADS * HEAD_DIM
BLK = 128
DILATIONS = (1, 4, 16)
CONV_WIDTH = 31
CHUNK = DILATIONS[-1] * BLK
NEG = -1e30
Q_SCALE = HEAD_DIM ** -0.5 * 1.4426950408889634

VMEM_LIMIT = 56 * 1024 * 1024


def _cparams(sem):
    return pltpu.CompilerParams(dimension_semantics=sem, vmem_limit_bytes=VMEM_LIMIT)


def _rms(x, g):
    return x * lax.rsqrt(jnp.mean(x * x, axis=-1, keepdims=True) + EPS) * g


NSLAB = 4
SLAB_CHUNK = CHUNK // NSLAB
PIECE = BLK // NSLAB


def _attn_kernel(q_ref, k_ref, v_ref, o_ref, qf, kf, vf, num, mm, ll, stage, onat, bias, *, seq):
    cp = NSLAB * BLK
    n_chunks = seq // CHUNK

    def cast_piece(i):
        t0 = pl.multiple_of(i * cp, cp)
        s0 = pl.multiple_of(i * BLK, BLK)
        for j, (src, dst) in enumerate(((q_ref, qf), (k_ref, kf), (v_ref, vf))):
            stage[j] = src[pl.ds(t0, cp), :].astype(F32)
            for r in range(NSLAB):
                dst[r, pl.ds(s0, BLK), :] = stage[j, pl.ds(r, BLK, stride=NSLAB), :]

    def cast_body(i, _):
        cast_piece(i)
        return 0

    lax.fori_loop(0, CHUNK // cp, cast_body, 0)

    qi = lax.broadcasted_iota(jnp.int32, (BLK, 2 * BLK), 0)
    kj = lax.broadcasted_iota(jnp.int32, (BLK, 2 * BLK), 1)

    def put(slot, dist):
        bias[slot] = jnp.where((dist >= 0) & (dist <= BLK), 0.0, NEG).astype(F32)

    put(0, qi + BLK - kj)
    put(1, qi - kj)
    d1 = NSLAB * (qi % PIECE - kj % (2 * PIECE)) + (qi // PIECE - kj // (2 * PIECE))
    put(2, d1 + BLK)
    put(3, d1)

    ones = jnp.ones((2 * BLK, HEAD_DIM), BF16)

    def block(q, k, v, bias_slot):
        s = lax.dot_general(q, k, (((1,), (1,)), ((), ())), preferred_element_type=F32)
        s = s + bias[bias_slot]
        m = jnp.max(s, axis=-1, keepdims=True)
        e = jnp.exp2(s - m).astype(BF16)
        pv = jnp.dot(e, jnp.concatenate([v, ones], axis=1), preferred_element_type=F32)
        return pv[:, :HEAD_DIM], jnp.broadcast_to(m, (BLK, HEAD_DIM)), pv[:, HEAD_DIM:]

    def chunk_body(c, _):
        r0 = pl.multiple_of(c * SLAB_CHUNK, SLAB_CHUNK)
        first_chunk = jnp.where(c == 0, 1, 0)

        nxt = jnp.minimum(c + 1, n_chunks - 1)
        for p in range(CHUNK // cp):
            cast_piece(nxt * (CHUNK // cp) + p)

        for mres in range(NSLAB):
            k0 = jnp.maximum(r0 - SLAB_CHUNK, 0) + mres
            for r in range(NSLAB):
                q = qf[r, pl.ds(r0 + mres, BLK, stride=NSLAB), :].astype(BF16)
                k = kf[r, pl.ds(k0, 2 * BLK, stride=NSLAB), :].astype(BF16)
                v = vf[r, pl.ds(k0, 2 * BLK, stride=NSLAB), :].astype(BF16)
                pv, m, l = block(q, k, v, first_chunk)
                dst = pl.ds(mres, BLK, stride=NSLAB)
                num[r, dst, :] = pv
                mm[r, dst, :] = m
                ll[r, dst, :] = l

        for nb in range(SLAB_CHUNK // BLK):
            loc = nb * BLK
            q0 = pl.multiple_of(r0 + loc, BLK)
            first = jnp.where(q0 == 0, 1, 0)
            k0 = pl.multiple_of(jnp.maximum(q0 - BLK, 0), BLK)
            for r in range(NSLAB):
                q = qf[r, pl.ds(q0, BLK), :].astype(BF16)
                k = kf[r, pl.ds(k0, 2 * BLK), :].astype(BF16)
                v = vf[r, pl.ds(k0, 2 * BLK), :].astype(BF16)
                pv, m, l = block(q, k, v, first)
                dst = pl.ds(loc, BLK)
                num[NSLAB + r, dst, :] = pv
                mm[NSLAB + r, dst, :] = m
                ll[NSLAB + r, dst, :] = l

        def combine(g):
            rows = pl.ds(g * BLK, BLK)
            for r in range(NSLAB):
                m0, m1, m2 = mm[r, rows, :], mm[NSLAB + r, rows, :], mm[2 * NSLAB + r, rows, :]
                mx = jnp.maximum(jnp.maximum(m0, m1), m2)
                a0, a1, a2 = jnp.exp2(m0 - mx), jnp.exp2(m1 - mx), jnp.exp2(m2 - mx)
                den = a0 * ll[r, rows, :] + a1 * ll[NSLAB + r, rows, :] + a2 * ll[2 * NSLAB + r, rows, :]
                o = (a0 * num[r, rows, :] + a1 * num[NSLAB + r, rows, :]
                     + a2 * num[2 * NSLAB + r, rows, :]) / den
                onat[g, pl.ds(r, BLK, stride=NSLAB), :] = o
            t0 = pl.multiple_of(c * CHUNK + g * (NSLAB * BLK), NSLAB * BLK)
            o_ref[pl.ds(t0, NSLAB * BLK), :] = onat[g].astype(o_ref.dtype)

        for nb in range(SLAB_CHUNK // PIECE):
            loc = nb * PIECE
            q0 = pl.multiple_of(r0 + loc, PIECE)
            first = jnp.where(q0 == 0, 1, 0)
            k0 = pl.multiple_of(jnp.maximum(q0 - PIECE, 0), PIECE)
            gather = lambda ref, start, n: jnp.concatenate(
                [ref[r, pl.ds(start, n), :] for r in range(NSLAB)], axis=0).astype(BF16)
            q = gather(qf, q0, PIECE)
            k = gather(kf, k0, 2 * PIECE)
            v = gather(vf, k0, 2 * PIECE)
            pv, m, l = block(q, k, v, 2 + first)
            for r in range(NSLAB):
                dst = pl.ds(loc, PIECE)
                src = slice(r * PIECE, (r + 1) * PIECE)
                num[2 * NSLAB + r, dst, :] = pv[src]
                mm[2 * NSLAB + r, dst, :] = m[src]
                ll[2 * NSLAB + r, dst, :] = l[src]
            if (loc + PIECE) % BLK == 0:
                combine(loc // BLK)
        return 0

    lax.fori_loop(0, n_chunks, chunk_body, 0)


def _attention(z3):
    B, S, _ = z3.shape
    rows4 = S // NSLAB
    blk = lambda off: pl.BlockSpec((None, S, HEAD_DIM), lambda b, h, off=off: (b, 0, off + h))
    return pl.pallas_call(
        functools.partial(_attn_kernel, seq=S),
        grid=(B, N_HEADS),
        in_specs=[blk(0), blk(N_HEADS), blk(2 * N_HEADS)],
        out_specs=pl.BlockSpec((None, S, HEAD_DIM), lambda b, h: (b, 0, h)),
        out_shape=jax.ShapeDtypeStruct((B, S, D_ATTN), BF16),
        scratch_shapes=[
            pltpu.VMEM((NSLAB, rows4, HEAD_DIM), F32),
            pltpu.VMEM((NSLAB, rows4, HEAD_DIM), F32),
            pltpu.VMEM((NSLAB, rows4, HEAD_DIM), F32),
            pltpu.VMEM((3 * NSLAB, SLAB_CHUNK, HEAD_DIM), F32),
            pltpu.VMEM((3 * NSLAB, SLAB_CHUNK, HEAD_DIM), F32),
            pltpu.VMEM((3 * NSLAB, SLAB_CHUNK, HEAD_DIM), F32),
            pltpu.VMEM((3, NSLAB * BLK, HEAD_DIM), F32),
            pltpu.VMEM((SLAB_CHUNK // BLK, NSLAB * BLK, HEAD_DIM), F32),
            pltpu.VMEM((4, BLK, 2 * BLK), F32),
        ],
        compiler_params=_cparams(("parallel", "parallel")),
        name="dilated_attn",
    )(z3, z3, z3)


SUBLANES = 8
LANES = 128
HALO = 32
CONV_ROWS = 64
PIECE_COLS = 256
NORM_ROWS = 16


def _fused_in_kernel(*refs, n_tiles, tiles_per_seq, groups_per_step, n_cast):
    x_ref, g_ref, w_ref, wdw_ref, bdw_ref, gln_ref, bln_ref, zeros_ref = refs[:8]
    cast_src = refs[8:8 + n_cast]
    zq_ref, oc_ref = refs[8 + n_cast:10 + n_cast]
    cast_dst = refs[10 + n_cast:10 + 2 * n_cast]
    a_ref, zs, ubuf, yrow = refs[10 + 2 * n_cast:]
    i, j = pl.program_id(0), pl.program_id(1)
    nj = pl.num_programs(1)
    tm, C = zs.shape
    cur, prev = i % 2, (i + 1) % 2
    off = HALO - (CONV_WIDTH - 1)
    win_rows = CONV_ROWS + HALO

    @pl.when((i == 0) & (j == 0))
    def _():
        ubuf[...] = jnp.zeros_like(ubuf)

    @pl.when(j == 0)
    def _():
        a_ref[...] = _rms(x_ref[...], g_ref[...]).astype(BF16)

    n_blocks = groups_per_step * (C // LANES)

    def conv_block(it, start=None):
        k = it // (C // LANES)
        c0 = (it % (C // LANES)) * LANES
        if not isinstance(it, int):
            c0 = pl.multiple_of(c0, LANES)
        t0 = pl.multiple_of((j * groups_per_step + k) * CONV_ROWS, CONV_ROWS)
        win = ubuf[prev, pl.ds(t0, win_rows), pl.ds(c0, LANES)]
        acc = (jnp.zeros((CONV_ROWS, LANES), F32) if start is None
               else jnp.tile(start, (CONV_ROWS // SUBLANES, 1)))
        for s in range(SUBLANES):
            ys = win if s == 0 else pltpu.roll(win, win_rows - s, axis=0)
            for blk8 in range(HALO // SUBLANES + 1):
                tap = SUBLANES * blk8 + s - off
                if 0 <= tap < CONV_WIDTH:
                    acc = acc + (ys[SUBLANES * blk8:SUBLANES * blk8 + CONV_ROWS]
                                 * wdw_ref[pl.ds(tap, 1), pl.ds(c0, LANES)])
        yrow[k, :, pl.ds(c0, LANES)] = acc
        return acc

    def zero_of(value):
        return (pltpu.bitcast(value[:SUBLANES, :LANES], jnp.uint32) & zeros_ref[...]).astype(F32)

    def norm_rows():
        for k in range(groups_per_step):
            t0 = pl.multiple_of((j * groups_per_step + k) * CONV_ROWS, CONV_ROWS)
            for r in range(0, CONV_ROWS, NORM_ROWS):
                y = yrow[k, r:r + NORM_ROWS, :] + bdw_ref[...]
                mu = jnp.mean(y, axis=-1, keepdims=True)
                yc = y - mu
                var = jnp.mean(yc * yc, axis=-1, keepdims=True)
                yn = yc * lax.rsqrt(var + EPS) * gln_ref[...] + bln_ref[...]
                oc_ref[pl.ds(t0 + r, NORM_ROWS), :] = (yn * jax.nn.sigmoid(yn)).astype(oc_ref.dtype)

    @pl.when((j < nj - 1) & (i < n_tiles))
    def _():
        scale = jnp.where(j == 0, Q_SCALE, 1.0)
        lead = 2 * SUBLANES
        n_half = C // 2
        for it in range(n_blocks):
            zero = jnp.tile(zero_of(conv_block(it)).astype(BF16), (lead // SUBLANES, PIECE_COLS // LANES))
            k0, n0 = (it // 2) * PIECE_COLS, (it % 2) * n_half
            lhs = jnp.concatenate([a_ref[:lead, k0:k0 + PIECE_COLS] + zero, a_ref[lead:, k0:k0 + PIECE_COLS]],
                                  axis=0)
            part = jnp.dot(lhs, w_ref[k0:k0 + PIECE_COLS, n0:n0 + n_half], preferred_element_type=F32)
            zs[:, n0:n0 + n_half] = part if it < 2 else zs[:, n0:n0 + n_half] + part

        @pl.when(j < nj - 2)
        def _():
            zq_ref[...] = (zs[...] * scale).astype(zq_ref.dtype)

        norm_rows()

    @pl.when((j < nj - 1) & (i == n_tiles))
    def _():
        def body(it, _):
            conv_block(it)
            return 0

        lax.fori_loop(0, n_blocks, body, 0)
        norm_rows()

    @pl.when((j == nj - 1) & (i < n_tiles))
    def _():
        gate = jnp.dot(a_ref[...], w_ref[...], preferred_element_type=F32)
        first = (i % tiles_per_seq) == 0
        ubuf[cur, pl.ds(0, HALO), :] = jnp.where(first, 0.0, ubuf[prev, pl.ds(tm, HALO), :])
        ubuf[cur, pl.ds(HALO, tm), :] = zs[...] * jax.nn.sigmoid(gate)
        for src, dst in zip(cast_src, cast_dst):
            dst[...] = src[...].astype(dst.dtype)


def _fused_in(x2, g, w, w_dw, b_dw, g_ln, b_ln, seq, cast_weights, tm=512):
    M, D = x2.shape
    N = w.shape[1]
    C = w_dw.shape[1]
    tn = D_ATTN
    nj = N // tn
    n_tiles = M // tm
    groups = tm // CONV_ROWS
    assert C == tn and N == nj * tn and groups % (nj - 1) == 0 and seq % tm == 0
    last = n_tiles - 1
    vec = pl.BlockSpec((1, C), lambda i, j: (0, 0))
    def slab(wt, arrive=0):
        return pl.BlockSpec((wt.shape[0] // n_tiles, wt.shape[1]),
                            lambda i, j: (jnp.clip(i - jnp.where(j < arrive, 1, 0), 0, last), 0))

    arrivals = [1 + k % (nj - 1) for k in range(len(cast_weights))]
    assert all(wt.shape[0] % (16 * n_tiles) == 0 for wt in cast_weights)
    return pl.pallas_call(
        functools.partial(_fused_in_kernel, n_tiles=n_tiles, tiles_per_seq=seq // tm,
                          groups_per_step=groups // (nj - 1), n_cast=len(cast_weights)),
        grid=(n_tiles + 1, nj),
        in_specs=[
            pl.BlockSpec((tm, D), lambda i, j: (jnp.minimum(i, last), 0)),
            pl.BlockSpec((1, D), lambda i, j: (0, 0)),
            pl.BlockSpec((D, tn), lambda i, j: (0, j)),
            pl.BlockSpec((CONV_WIDTH, C), lambda i, j: (0, 0)),
            vec, vec, vec,
            pl.BlockSpec((SUBLANES, LANES), lambda i, j: (0, 0)),
        ] + [slab(wt, s) for wt, s in zip(cast_weights, arrivals)],
        out_specs=[
            pl.BlockSpec((tm, tn), lambda i, j: (jnp.minimum(i, last),
                                                 jnp.where(i > last, nj - 3, jnp.minimum(j, nj - 3)))),
            pl.BlockSpec((tm, C), lambda i, j: (jnp.maximum(i - 1, 0), 0)),
        ] + [slab(wt) for wt in cast_weights],
        out_shape=[jax.ShapeDtypeStruct((M, (nj - 2) * tn), BF16), jax.ShapeDtypeStruct((M, C), BF16)]
        + [jax.ShapeDtypeStruct(wt.shape, BF16) for wt in cast_weights],
        scratch_shapes=[
            pltpu.VMEM((tm, D), BF16),
            pltpu.VMEM((tm, C), F32),
            pltpu.VMEM((2, HALO + tm, C), F32),
            pltpu.VMEM((groups // (nj - 1), CONV_ROWS, C), F32),
        ],
        compiler_params=_cparams(("arbitrary", "arbitrary")),
        name="in_proj_conv",
    )(x2, g, w, w_dw, b_dw, g_ln, b_ln, jnp.zeros((SUBLANES, LANES), jnp.uint32), *cast_weights)


def _out_proj_kernel(oa_ref, oc_ref, x_ref, wa_ref, wc_ref, g_ref, h_ref, f_ref):
    acc = jnp.dot(oa_ref[...], wa_ref[...], preferred_element_type=F32)
    acc = acc + jnp.dot(oc_ref[...], wc_ref[...], preferred_element_type=F32)
    h = x_ref[...] + acc
    h_ref[...] = h
    f_ref[...] = _rms(h, g_ref[...]).astype(f_ref.dtype)


def _out_proj(oa, oc, x2, w_out, g, tm=512):
    M, D = x2.shape
    Ka, Kc = oa.shape[1], oc.shape[1]
    row = lambda n: pl.BlockSpec((tm, n), lambda i: (i, 0))
    return pl.pallas_call(
        _out_proj_kernel,
        grid=(M // tm,),
        in_specs=[
            row(Ka), row(Kc), row(D),
            pl.BlockSpec((Ka, D), lambda i: (0, 0)),
            pl.BlockSpec((Kc, D), lambda i: (Ka // Kc, 0)),
            pl.BlockSpec((1, D), lambda i: (0, 0)),
        ],
        out_specs=[row(D), row(D)],
        out_shape=[jax.ShapeDtypeStruct((M, D), F32), jax.ShapeDtypeStruct((M, D), BF16)],
        compiler_params=_cparams(("parallel",)),
        name="out_proj",
    )(oa, oc, x2, w_out, w_out, g)


def _ffn_kernel(f_ref, wg_ref, wu_ref, wd_ref, o_ref):
    @pl.when(pl.program_id(1) == 0)
    def _():
        o_ref[...] = jnp.zeros_like(o_ref)

    f = f_ref[...]
    g = jnp.dot(f, wg_ref[...], preferred_element_type=F32)
    u = jnp.dot(f, wu_ref[...], preferred_element_type=F32)
    act = (g * jax.nn.sigmoid(g) * u).astype(BF16)
    o_ref[...] += jnp.dot(act, wd_ref[...], preferred_element_type=F32)


def _ffn(f, wg, wu, wd, tm=1024, tf=512):
    M, D = f.shape
    F = wg.shape[1]
    return pl.pallas_call(
        _ffn_kernel,
        grid=(M // tm, F // tf),
        in_specs=[
            pl.BlockSpec((tm, D), lambda i, j: (i, 0)),
            pl.BlockSpec((D, tf), lambda i, j: (0, j)),
            pl.BlockSpec((D, tf), lambda i, j: (0, j)),
            pl.BlockSpec((tf, D), lambda i, j: (j, 0)),
        ],
        out_specs=pl.BlockSpec((tm, D), lambda i, j: (i, 0)),
        out_shape=jax.ShapeDtypeStruct((M, D), F32),
        compiler_params=_cparams(("parallel", "arbitrary")),
        name="ffn",
    )(f, wg, wu, wd)


def _ple_kernel(h_ref, d_ref, p_ref, wpg_ref, bpg_ref, wple_ref, gp_ref, gf_ref, o_ref, *, final):
    h2 = h_ref[...] + d_ref[...]
    a = _rms(h2, gp_ref[...]).astype(BF16)
    gte = jax.nn.sigmoid(jnp.dot(a, wpg_ref[...], preferred_element_type=F32) + bpg_ref[...])
    ple = jnp.dot(p_ref[...].astype(BF16), wple_ref[...], preferred_element_type=F32)
    h3 = h2 + ple * gte
    o_ref[...] = _rms(h3, gf_ref[...]) if final else h3


def _ple(h1, d, p2, wpg, bpg, wple, g_ple, g_final, final, tm=512):
    M, D = h1.shape
    P = p2.shape[1]
    row = lambda n: pl.BlockSpec((tm, n), lambda i: (i, 0))
    full = lambda r, c: pl.BlockSpec((r, c), lambda i: (0, 0))
    return pl.pallas_call(
        functools.partial(_ple_kernel, final=final),
        grid=(M // tm,),
        in_specs=[row(D), row(D), row(P), full(D, D), full(1, D), full(P, D), full(1, D), full(1, D)],
        out_specs=row(D),
        out_shape=jax.ShapeDtypeStruct((M, D), F32),
        compiler_params=_cparams(("parallel",)),
        name="ple_final",
    )(h1, d, p2, wpg, bpg, wple, g_ple, g_final)


def kernel(x, p, g_mix, w_in, w_dw, b_dw, g_conv_ln, b_conv_ln, w_out, g_ffn, w_gate, w_up, w_down,
           g_ple, w_pgate, b_pgate, w_ple, g_final):
    B, S, D = x.shape
    depth = w_in.shape[0]
    M = B * S
    h = x.reshape(M, D)
    out = None
    for i in range(depth):
        last = i == depth - 1
        zq, oc, wo, wpg = _fused_in(
            h, g_mix[i][None], w_in[i].astype(BF16), w_dw[i, :, 0, :], b_dw[i][None], g_conv_ln[i][None],
            b_conv_ln[i][None], S, (w_out[i], w_pgate[i]))
        oa = _attention(zq.reshape(B, S, -1)).reshape(M, -1)
        h1, f = _out_proj(oa, oc, h, wo, g_ffn[i][None])
        d = _ffn(f, w_gate[i].astype(BF16), w_up[i].astype(BF16), w_down[i].astype(BF16))
        out = _ple(h1, d, p[i].reshape(M, -1), wpg, b_pgate[i][None],
                   w_ple[i].astype(BF16), g_ple[i][None], g_final[None], last)
        h = out
    return out.reshape(B, S, D)
```

```python
import functools

import jax
import jax.numpy as jnp
from jax import lax
from jax.experimental import pallas as pl
from jax.experimental.pallas import tpu as pltpu

F32 = jnp.float32
BF16 = jnp.bfloat16

EPS = 1e-6
N_HEADS = 8
HEAD_DIM = 128
D_ATTN = N_HEADS * HEAD_DIM
BLK = 128
DILATIONS = (1, 4, 16)
CONV_WIDTH = 31
CHUNK = DILATIONS[-1] * BLK
NEG = -1e30
Q_SCALE = HEAD_DIM ** -0.5 * 1.4426950408889634

VMEM_LIMIT = 56 * 1024 * 1024


def _cparams(sem):
    return pltpu.CompilerParams(dimension_semantics=sem, vmem_limit_bytes=VMEM_LIMIT)


def _rms(x, g):
    return x * lax.rsqrt(jnp.mean(x * x, axis=-1, keepdims=True) + EPS) * g


NSLAB = 4
SLAB_CHUNK = CHUNK // NSLAB
PIECE = BLK // NSLAB


def _attn_kernel(q_ref, k_ref, v_ref, o_ref, qf, kf, vf, num, mm, ll, stage, onat, bias, *, seq):
    cp = NSLAB * BLK
    n_chunks = seq // CHUNK

    def cast_piece(i):
        t0 = pl.multiple_of(i * cp, cp)
        s0 = pl.multiple_of(i * BLK, BLK)
        for j, (src, dst) in enumerate(((q_ref, qf), (k_ref, kf), (v_ref, vf))):
            stage[j] = src[pl.ds(t0, cp), :].astype(F32)
            for r in range(NSLAB):
                dst[r, pl.ds(s0, BLK), :] = stage[j, pl.ds(r, BLK, stride=NSLAB), :]

    def cast_body(i, _):
        cast_piece(i)
        return 0

    lax.fori_loop(0, CHUNK // cp, cast_body, 0)

    qi = lax.broadcasted_iota(jnp.int32, (BLK, 2 * BLK), 0)
    kj = lax.broadcasted_iota(jnp.int32, (BLK, 2 * BLK), 1)

    def put(slot, dist):
        bias[slot] = jnp.where((dist >= 0) & (dist <= BLK), 0.0, NEG).astype(F32)

    put(0, qi + BLK - kj)
    put(1, qi - kj)
    d1 = NSLAB * (qi % PIECE - kj % (2 * PIECE)) + (qi // PIECE - kj // (2 * PIECE))
    put(2, d1 + BLK)
    put(3, d1)

    ones = jnp.ones((2 * BLK, HEAD_DIM), BF16)

    def block(q, k, v, bias_slot):
        s = lax.dot_general(q, k, (((1,), (1,)), ((), ())), preferred_element_type=F32)
        s = s + bias[bias_slot]
        m = jnp.max(s, axis=-1, keepdims=True)
        e = jnp.exp2(s - m).astype(BF16)
        pv = jnp.dot(e, jnp.concatenate([v, ones], axis=1), preferred_element_type=F32)
        return pv[:, :HEAD_DIM], jnp.broadcast_to(m, (BLK, HEAD_DIM)), pv[:, HEAD_DIM:]

    def chunk_body(c, _):
        r0 = pl.multiple_of(c * SLAB_CHUNK, SLAB_CHUNK)
        first_chunk = jnp.where(c == 0, 1, 0)

        nxt = jnp.minimum(c + 1, n_chunks - 1)
        for p in range(CHUNK // cp):
            cast_piece(nxt * (CHUNK // cp) + p)

        for mres in range(NSLAB):
            k0 = jnp.maximum(r0 - SLAB_CHUNK, 0) + mres
            for r in range(NSLAB):
                q = qf[r, pl.ds(r0 + mres, BLK, stride=NSLAB), :].astype(BF16)
                k = kf[r, pl.ds(k0, 2 * BLK, stride=NSLAB), :].astype(BF16)
                v = vf[r, pl.ds(k0, 2 * BLK, stride=NSLAB), :].astype(BF16)
                pv, m, l = block(q, k, v, first_chunk)
                dst = pl.ds(mres, BLK, stride=NSLAB)
                num[r, dst, :] = pv
                mm[r, dst, :] = m
                ll[r, dst, :] = l

        for nb in range(SLAB_CHUNK // BLK):
            loc = nb * BLK
            q0 = pl.multiple_of(r0 + loc, BLK)
            first = jnp.where(q0 == 0, 1, 0)
            k0 = pl.multiple_of(jnp.maximum(q0 - BLK, 0), BLK)
            for r in range(NSLAB):
                q = qf[r, pl.ds(q0, BLK), :].astype(BF16)
                k = kf[r, pl.ds(k0, 2 * BLK), :].astype(BF16)
                v = vf[r, pl.ds(k0, 2 * BLK), :].astype(BF16)
                pv, m, l = block(q, k, v, first)
                dst = pl.ds(loc, BLK)
                num[NSLAB + r, dst, :] = pv
                mm[NSLAB + r, dst, :] = m
                ll[NSLAB + r, dst, :] = l

        def combine(g):
            rows = pl.ds(g * BLK, BLK)
            for r in range(NSLAB):
                m0, m1, m2 = mm[r, rows, :], mm[NSLAB + r, rows, :], mm[2 * NSLAB + r, rows, :]
                mx = jnp.maximum(jnp.maximum(m0, m1), m2)
                a0, a1, a2 = jnp.exp2(m0 - mx), jnp.exp2(m1 - mx), jnp.exp2(m2 - mx)
                den = a0 * ll[r, rows, :] + a1 * ll[NSLAB + r, rows, :] + a2 * ll[2 * NSLAB + r, rows, :]
                o = (a0 * num[r, rows, :] + a1 * num[NSLAB + r, rows, :]
                     + a2 * num[2 * NSLAB + r, rows, :]) / den
                onat[g, pl.ds(r, BLK, stride=NSLAB), :] = o
            t0 = pl.multiple_of(c * CHUNK + g * (NSLAB * BLK), NSLAB * BLK)
            o_ref[pl.ds(t0, NSLAB * BLK), :] = onat[g].astype(o_ref.dtype)

        for nb in range(SLAB_CHUNK // PIECE):
            loc = nb * PIECE
            q0 = pl.multiple_of(r0 + loc, PIECE)
            first = jnp.where(q0 == 0, 1, 0)
            k0 = pl.multiple_of(jnp.maximum(q0 - PIECE, 0), PIECE)
            gather = lambda ref, start, n: jnp.concatenate(
                [ref[r, pl.ds(start, n), :] for r in range(NSLAB)], axis=0).astype(BF16)
            q = gather(qf, q0, PIECE)
            k = gather(kf, k0, 2 * PIECE)
            v = gather(vf, k0, 2 * PIECE)
            pv, m, l = block(q, k, v, 2 + first)
            for r in range(NSLAB):
                dst = pl.ds(loc, PIECE)
                src = slice(r * PIECE, (r + 1) * PIECE)
                num[2 * NSLAB + r, dst, :] = pv[src]
                mm[2 * NSLAB + r, dst, :] = m[src]
                ll[2 * NSLAB + r, dst, :] = l[src]
            if (loc + PIECE) % BLK == 0:
                combine(loc // BLK)
        return 0

    lax.fori_loop(0, n_chunks, chunk_body, 0)


def _attention(z3):
    B, S, _ = z3.shape
    rows4 = S // NSLAB
    blk = lambda off: pl.BlockSpec((None, S, HEAD_DIM), lambda b, h, off=off: (b, 0, off + h))
    return pl.pallas_call(
        functools.partial(_attn_kernel, seq=S),
        grid=(B, N_HEADS),
        in_specs=[blk(0), blk(N_HEADS), blk(2 * N_HEADS)],
        out_specs=pl.BlockSpec((None, S, HEAD_DIM), lambda b, h: (b, 0, h)),
        out_shape=jax.ShapeDtypeStruct((B, S, D_ATTN), BF16),
        scratch_shapes=[
            pltpu.VMEM((NSLAB, rows4, HEAD_DIM), F32),
            pltpu.VMEM((NSLAB, rows4, HEAD_DIM), F32),
            pltpu.VMEM((NSLAB, rows4, HEAD_DIM), F32),
            pltpu.VMEM((3 * NSLAB, SLAB_CHUNK, HEAD_DIM), F32),
            pltpu.VMEM((3 * NSLAB, SLAB_CHUNK, HEAD_DIM), F32),
            pltpu.VMEM((3 * NSLAB, SLAB_CHUNK, HEAD_DIM), F32),
            pltpu.VMEM((3, NSLAB * BLK, HEAD_DIM), F32),
            pltpu.VMEM((SLAB_CHUNK // BLK, NSLAB * BLK, HEAD_DIM), F32),
            pltpu.VMEM((4, BLK, 2 * BLK), F32),
        ],
        compiler_params=_cparams(("parallel", "parallel")),
        name="dilated_attn",
    )(z3, z3, z3)


SUBLANES = 8
LANES = 128
HALO = 32
CONV_ROWS = 64
PIECE_COLS = 256
NORM_ROWS = 16


def _fused_in_kernel(*refs, n_tiles, tiles_per_seq, groups_per_step, n_cast):
    x_ref, g_ref, w_ref, wdw_ref, bdw_ref, gln_ref, bln_ref, zeros_ref = refs[:8]
    cast_src = refs[8:8 + n_cast]
    zq_ref, oc_ref = refs[8 + n_cast:10 + n_cast]
    cast_dst = refs[10 + n_cast:10 + 2 * n_cast]
    a_ref, zs, gs, ubuf, yrow = refs[10 + 2 * n_cast:]
    i, j = pl.program_id(0), pl.program_id(1)
    nj = pl.num_programs(1)
    tm, C = zs.shape
    off = HALO - (CONV_WIDTH - 1)
    win_rows = CONV_ROWS + HALO

    @pl.when((i == 0) & (j == 0))
    def _():
        ubuf[...] = jnp.zeros_like(ubuf)
        a_ref[0] = _rms(x_ref[...], g_ref[...]).astype(BF16)

    n_blocks = groups_per_step * (C // LANES)

    def conv_block(it, prev):
        k = it // (C // LANES)
        c0 = (it % (C // LANES)) * LANES
        if not isinstance(it, int):
            c0 = pl.multiple_of(c0, LANES)
        t0 = pl.multiple_of((j * groups_per_step + k) * CONV_ROWS, CONV_ROWS)
        win = ubuf[prev, pl.ds(t0, win_rows), pl.ds(c0, LANES)]
        acc = jnp.zeros((CONV_ROWS, LANES), F32)
        for s in range(SUBLANES):
            ys = win if s == 0 else pltpu.roll(win, win_rows - s, axis=0)
            for blk8 in range(HALO // SUBLANES + 1):
                tap = SUBLANES * blk8 + s - off
                if 0 <= tap < CONV_WIDTH:
                    acc = acc + (ys[SUBLANES * blk8:SUBLANES * blk8 + CONV_ROWS]
                                 * wdw_ref[pl.ds(tap, 1), pl.ds(c0, LANES)])
        yrow[k, :, pl.ds(c0, LANES)] = acc
        return acc

    def zero_of(value):
        return (pltpu.bitcast(value[:SUBLANES, :LANES], jnp.uint32) & zeros_ref[...]).astype(F32)

    def norm_rows():
        for k in range(groups_per_step):
            t0 = pl.multiple_of((j * groups_per_step + k) * CONV_ROWS, CONV_ROWS)
            for r in range(0, CONV_ROWS, NORM_ROWS):
                y = yrow[k, r:r + NORM_ROWS, :] + bdw_ref[...]
                mu = jnp.mean(y, axis=-1, keepdims=True)
                yc = y - mu
                var = jnp.mean(yc * yc, axis=-1, keepdims=True)
                yn = yc * lax.rsqrt(var + EPS) * gln_ref[...] + bln_ref[...]
                oc_ref[pl.ds(t0 + r, NORM_ROWS), :] = (yn * jax.nn.sigmoid(yn)).astype(oc_ref.dtype)

    def matmul_piece(cur, zero, k0, n0):
        lhs = jnp.concatenate([a_ref[cur, :2 * SUBLANES, k0:k0 + PIECE_COLS] + zero,
                               a_ref[cur, 2 * SUBLANES:, k0:k0 + PIECE_COLS]], axis=0)
        return jnp.dot(lhs, w_ref[k0:k0 + PIECE_COLS, n0:n0 + C // 2], preferred_element_type=F32)

    def link(value):
        return jnp.tile(zero_of(value).astype(BF16), (2, PIECE_COLS // LANES))

    n_half = C // 2

    def pieces(acc_ref, cur, vpu_work):
        for it in range(n_blocks):
            zero = link(vpu_work(it))
            k0, n0 = (it // 2) * PIECE_COLS, (it % 2) * n_half
            part = matmul_piece(cur, zero, k0, n0)
            acc_ref[:, n0:n0 + n_half] = part if it < 2 else acc_ref[:, n0:n0 + n_half] + part

    for cur, prev in ((0, 1), (1, 0)):
        on = (i % 2) == cur

        @pl.when(on & (j < nj - 1) & (i < n_tiles))
        def _(cur=cur, prev=prev):
            pieces(zs, cur, lambda it: conv_block(it, prev))
            norm_rows()

            @pl.when(j < nj - 2)
            def _():
                zq_ref[...] = (zs[...] * jnp.where(j == 0, Q_SCALE, 1.0)).astype(zq_ref.dtype)

        @pl.when(on & (j < nj - 1) & (i == n_tiles))
        def _(prev=prev):
            def body(it, _):
                conv_block(it, prev)
                return 0

            lax.fori_loop(0, n_blocks, body, 0)
            norm_rows()

        @pl.when(on & (j == nj - 1) & (i < n_tiles))
        def _(cur=cur, prev=prev):
            chunk = tm // n_blocks

            def norm_chunk(it):
                rows = pl.ds(it * chunk, chunk)
                y = _rms(x_ref[rows, :], g_ref[...])
                a_ref[prev, rows, :] = y.astype(BF16)
                return y

            pieces(gs, cur, norm_chunk)
            first = (i % tiles_per_seq) == 0
            ubuf[cur, pl.ds(0, HALO), :] = jnp.where(first, 0.0, ubuf[prev, pl.ds(tm, HALO), :])
            ubuf[cur, pl.ds(HALO, tm), :] = zs[...] * jax.nn.sigmoid(gs[...])
            for src, dst in zip(cast_src, cast_dst):
                dst[...] = src[...].astype(dst.dtype)


def _fused_in(x2, g, w, w_dw, b_dw, g_ln, b_ln, seq, cast_weights, tm=512):
    M, D = x2.shape
    N = w.shape[1]
    C = w_dw.shape[1]
    tn = D_ATTN
    nj = N // tn
    n_tiles = M // tm
    groups = tm // CONV_ROWS
    assert C == tn and N == nj * tn and groups % (nj - 1) == 0 and seq % tm == 0
    last = n_tiles - 1
    vec = pl.BlockSpec((1, C), lambda i, j: (0, 0))
    def slab(wt, arrive=0):
        return pl.BlockSpec((wt.shape[0] // n_tiles, wt.shape[1]),
                            lambda i, j: (jnp.clip(i - jnp.where(j < arrive, 1, 0), 0, last), 0))

    arrivals = [1 + k % (nj - 1) for k in range(len(cast_weights))]
    assert all(wt.shape[0] % (16 * n_tiles) == 0 for wt in cast_weights)
    return pl.pallas_call(
        functools.partial(_fused_in_kernel, n_tiles=n_tiles, tiles_per_seq=seq // tm,
                          groups_per_step=groups // (nj - 1), n_cast=len(cast_weights)),
        grid=(n_tiles + 1, nj),
        in_specs=[
            pl.BlockSpec((tm, D), lambda i, j: (jnp.where((i == 0) & (j < nj - 1), 0, jnp.minimum(i + 1, last)), 0)),
            pl.BlockSpec((1, D), lambda i, j: (0, 0)),
            pl.BlockSpec((D, tn), lambda i, j: (0, j)),
            pl.BlockSpec((CONV_WIDTH, C), lambda i, j: (0, 0)),
            vec, vec, vec,
            pl.BlockSpec((SUBLANES, LANES), lambda i, j: (0, 0)),
        ] + [slab(wt, s) for wt, s in zip(cast_weights, arrivals)],
        out_specs=[
            pl.BlockSpec((tm, tn), lambda i, j: (jnp.minimum(i, last),
                                                 jnp.where(i > last, nj - 3, jnp.minimum(j, nj - 3)))),
            pl.BlockSpec((tm, C), lambda i, j: (jnp.maximum(i - 1, 0), 0)),
        ] + [slab(wt) for wt in cast_weights],
        out_shape=[jax.ShapeDtypeStruct((M, (nj - 2) * tn), BF16), jax.ShapeDtypeStruct((M, C), BF16)]
        + [jax.ShapeDtypeStruct(wt.shape, BF16) for wt in cast_weights],
        scratch_shapes=[
            pltpu.VMEM((2, tm, D), BF16),
            pltpu.VMEM((tm, C), F32),
            pltpu.VMEM((tm, C), F32),
            pltpu.VMEM((2, HALO + tm, C), F32),
            pltpu.VMEM((groups // (nj - 1), CONV_ROWS, C), F32),
        ],
        compiler_params=_cparams(("arbitrary", "arbitrary")),
        name="in_proj_conv",
    )(x2, g, w, w_dw, b_dw, g_ln, b_ln, jnp.zeros((SUBLANES, LANES), jnp.uint32), *cast_weights)


def _out_proj_kernel(oa_ref, oc_ref, x_ref, wa_ref, wc_ref, g_ref, h_ref, f_ref):
    acc = jnp.dot(oa_ref[...], wa_ref[...], preferred_element_type=F32)
    acc = acc + jnp.dot(oc_ref[...], wc_ref[...], preferred_element_type=F32)
    h = x_ref[...] + acc
    h_ref[...] = h
    f_ref[...] = _rms(h, g_ref[...]).astype(f_ref.dtype)


def _out_proj(oa, oc, x2, w_out, g, tm=512):
    M, D = x2.shape
    Ka, Kc = oa.shape[1], oc.shape[1]
    row = lambda n: pl.BlockSpec((tm, n), lambda i: (i, 0))
    return pl.pallas_call(
        _out_proj_kernel,
        grid=(M // tm,),
        in_specs=[
            row(Ka), row(Kc), row(D),
            pl.BlockSpec((Ka, D), lambda i: (0, 0)),
            pl.BlockSpec((Kc, D), lambda i: (Ka // Kc, 0)),
            pl.BlockSpec((1, D), lambda i: (0, 0)),
        ],
        out_specs=[row(D), row(D)],
        out_shape=[jax.ShapeDtypeStruct((M, D), F32), jax.ShapeDtypeStruct((M, D), BF16)],
        compiler_params=_cparams(("parallel",)),
        name="out_proj",
    )(oa, oc, x2, w_out, w_out, g)


def _ffn_kernel(f_ref, wg_ref, wu_ref, wd_ref, o_ref):
    @pl.when(pl.program_id(1) == 0)
    def _():
        o_ref[...] = jnp.zeros_like(o_ref)

    f = f_ref[...]
    g = jnp.dot(f, wg_ref[...], preferred_element_type=F32)
    u = jnp.dot(f, wu_ref[...], preferred_element_type=F32)
    act = (g * jax.nn.sigmoid(g) * u).astype(BF16)
    o_ref[...] += jnp.dot(act, wd_ref[...], preferred_element_type=F32)


def _ffn(f, wg, wu, wd, tm=1024, tf=512):
    M, D = f.shape
    F = wg.shape[1]
    return pl.pallas_call(
        _ffn_kernel,
        grid=(M // tm, F // tf),
        in_specs=[
            pl.BlockSpec((tm, D), lambda i, j: (i, 0)),
            pl.BlockSpec((D, tf), lambda i, j: (0, j)),
            pl.BlockSpec((D, tf), lambda i, j: (0, j)),
            pl.BlockSpec((tf, D), lambda i, j: (j, 0)),
        ],
        out_specs=pl.BlockSpec((tm, D), lambda i, j: (i, 0)),
        out_shape=jax.ShapeDtypeStruct((M, D), F32),
        compiler_params=_cparams(("parallel", "arbitrary")),
        name="ffn",
    )(f, wg, wu, wd)


def _ple_kernel(h_ref, d_ref, p_ref, wpg_ref, bpg_ref, wple_ref, gp_ref, gf_ref, o_ref, *, final):
    h2 = h_ref[...] + d_ref[...]
    a = _rms(h2, gp_ref[...]).astype(BF16)
    gte = jax.nn.sigmoid(jnp.dot(a, wpg_ref[...], preferred_element_type=F32) + bpg_ref[...])
    ple = jnp.dot(p_ref[...].astype(BF16), wple_ref[...], preferred_element_type=F32)
    h3 = h2 + ple * gte
    o_ref[...] = _rms(h3, gf_ref[...]) if final else h3


def _ple(h1, d, p2, wpg, bpg, wple, g_ple, g_final, final, tm=512):
    M, D = h1.shape
    P = p2.shape[1]
    row = lambda n: pl.BlockSpec((tm, n), lambda i: (i, 0))
    full = lambda r, c: pl.BlockSpec((r, c), lambda i: (0, 0))
    return pl.pallas_call(
        functools.partial(_ple_kernel, final=final),
        grid=(M // tm,),
        in_specs=[row(D), row(D), row(P), full(D, D), full(1, D), full(P, D), full(1, D), full(1, D)],
        out_specs=row(D),
        out_shape=jax.ShapeDtypeStruct((M, D), F32),
        compiler_params=_cparams(("parallel",)),
        name="ple_final",
    )(h1, d, p2, wpg, bpg, wple, g_ple, g_final)


def kernel(x, p, g_mix, w_in, w_dw, b_dw, g_conv_ln, b_conv_ln, w_out, g_ffn, w_gate, w_up, w_down,
           g_ple, w_pgate, b_pgate, w_ple, g_final):
    B, S, D = x.shape
    depth = w_in.shape[0]
    M = B * S
    h = x.reshape(M, D)
    out = None
    for i in range(depth):
        last = i == depth - 1
        zq, oc, wo, wg, wu, wd, wpg = _fused_in(
            h, g_mix[i][None], w_in[i].astype(BF16), w_dw[i, :, 0, :], b_dw[i][None], g_conv_ln[i][None],
            b_conv_ln[i][None], S, (w_out[i], w_gate[i], w_up[i], w_down[i], w_pgate[i]))
        oa = _attention(zq.reshape(B, S, -1)).reshape(M, -1)
        h1, f = _out_proj(oa, oc, h, wo, g_ffn[i][None])
        d = _ffn(f, wg, wu, wd)
        out = _ple(h1, d, p[i].reshape(M, -1), wpg, b_pgate[i][None],
                   w_ple[i].astype(BF16), g_ple[i][None], g_final[None], last)
        h = out
    return out.reshape(B, S, D)
```

```python
import functools

import jax
import jax.numpy as jnp
from jax import lax
from jax.experimental import pallas as pl
from jax.experimental.pallas import tpu as pltpu

F32 = jnp.float32
BF16 = jnp.bfloat16

EPS = 1e-6
N_HEADS = 8
HEAD_DIM = 128
D_ATTN = N_HEADS * HEAD_DIM
BLK = 128
DILATIONS = (1, 4, 16)
CONV_WIDTH = 31
CHUNK = DILATIONS[-1] * BLK
NEG = -1e30
Q_SCALE = HEAD_DIM ** -0.5 * 1.4426950408889634

VMEM_LIMIT = 56 * 1024 * 1024


def _cparams(sem):
    return pltpu.CompilerParams(dimension_semantics=sem, vmem_limit_bytes=VMEM_LIMIT)


def _rms(x, g):
    return x * lax.rsqrt(jnp.mean(x * x, axis=-1, keepdims=True) + EPS) * g


NSLAB = 4
SLAB_CHUNK = CHUNK // NSLAB
PIECE = BLK // NSLAB


def _attn_kernel(q_ref, k_ref, v_ref, o_ref, qf, kf, vf, num, mm, ll, stage, onat, bias, *, seq):
    cp = NSLAB * BLK
    n_chunks = seq // CHUNK

    def cast_piece(i):
        t0 = pl.multiple_of(i * cp, cp)
        s0 = pl.multiple_of(i * BLK, BLK)
        for j, (src, dst) in enumerate(((q_ref, qf), (k_ref, kf), (v_ref, vf))):
            stage[j] = src[pl.ds(t0, cp), :].astype(F32)
            for r in range(NSLAB):
                dst[r, pl.ds(s0, BLK), :] = stage[j, pl.ds(r, BLK, stride=NSLAB), :]

    def cast_body(i, _):
        cast_piece(i)
        return 0

    lax.fori_loop(0, CHUNK // cp, cast_body, 0)

    qi = lax.broadcasted_iota(jnp.int32, (BLK, 2 * BLK), 0)
    kj = lax.broadcasted_iota(jnp.int32, (BLK, 2 * BLK), 1)

    def put(slot, dist):
        bias[slot] = jnp.where((dist >= 0) & (dist <= BLK), 0.0, NEG).astype(F32)

    put(0, qi + BLK - kj)
    put(1, qi - kj)
    d1 = NSLAB * (qi % PIECE - kj % (2 * PIECE)) + (qi // PIECE - kj // (2 * PIECE))
    put(2, d1 + BLK)
    put(3, d1)

    ones = jnp.ones((2 * BLK, HEAD_DIM), BF16)

    def block(q, k, v, bias_slot):
        s = lax.dot_general(q, k, (((1,), (1,)), ((), ())), preferred_element_type=F32)
        s = s + bias[bias_slot]
        m = jnp.max(s, axis=-1, keepdims=True)
        e = jnp.exp2(s - m).astype(BF16)
        pv = jnp.dot(e, jnp.concatenate([v, ones], axis=1), preferred_element_type=F32)
        return pv[:, :HEAD_DIM], jnp.broadcast_to(m, (BLK, HEAD_DIM)), pv[:, HEAD_DIM:]

    def chunk_body(c, _):
        r0 = pl.multiple_of(c * SLAB_CHUNK, SLAB_CHUNK)
        first_chunk = jnp.where(c == 0, 1, 0)

        nxt = jnp.minimum(c + 1, n_chunks - 1)
        for p in range(CHUNK // cp):
            cast_piece(nxt * (CHUNK // cp) + p)

        for mres in range(NSLAB):
            k0 = jnp.maximum(r0 - SLAB_CHUNK, 0) + mres
            for r in range(NSLAB):
                q = qf[r, pl.ds(r0 + mres, BLK, stride=NSLAB), :].astype(BF16)
                k = kf[r, pl.ds(k0, 2 * BLK, stride=NSLAB), :].astype(BF16)
                v = vf[r, pl.ds(k0, 2 * BLK, stride=NSLAB), :].astype(BF16)
                pv, m, l = block(q, k, v, first_chunk)
                dst = pl.ds(mres, BLK, stride=NSLAB)
                num[r, dst, :] = pv
                mm[r, dst, :] = m
                ll[r, dst, :] = l

        for nb in range(SLAB_CHUNK // BLK):
            loc = nb * BLK
            q0 = pl.multiple_of(r0 + loc, BLK)
            first = jnp.where(q0 == 0, 1, 0)
            k0 = pl.multiple_of(jnp.maximum(q0 - BLK, 0), BLK)
            for r in range(NSLAB):
                q = qf[r, pl.ds(q0, BLK), :].astype(BF16)
                k = kf[r, pl.ds(k0, 2 * BLK), :].astype(BF16)
                v = vf[r, pl.ds(k0, 2 * BLK), :].astype(BF16)
                pv, m, l = block(q, k, v, first)
                dst = pl.ds(loc, BLK)
                num[NSLAB + r, dst, :] = pv
                mm[NSLAB + r, dst, :] = m
                ll[NSLAB + r, dst, :] = l

        def combine(g):
            rows = pl.ds(g * BLK, BLK)
            for r in range(NSLAB):
                m0, m1, m2 = mm[r, rows, :], mm[NSLAB + r, rows, :], mm[2 * NSLAB + r, rows, :]
                mx = jnp.maximum(jnp.maximum(m0, m1), m2)
                a0, a1, a2 = jnp.exp2(m0 - mx), jnp.exp2(m1 - mx), jnp.exp2(m2 - mx)
                den = a0 * ll[r, rows, :] + a1 * ll[NSLAB + r, rows, :] + a2 * ll[2 * NSLAB + r, rows, :]
                o = (a0 * num[r, rows, :] + a1 * num[NSLAB + r, rows, :]
                     + a2 * num[2 * NSLAB + r, rows, :]) / den
                onat[g, pl.ds(r, BLK, stride=NSLAB), :] = o
            t0 = pl.multiple_of(c * CHUNK + g * (NSLAB * BLK), NSLAB * BLK)
            o_ref[pl.ds(t0, NSLAB * BLK), :] = onat[g].astype(o_ref.dtype)

        for nb in range(SLAB_CHUNK // PIECE):
            loc = nb * PIECE
            q0 = pl.multiple_of(r0 + loc, PIECE)
            first = jnp.where(q0 == 0, 1, 0)
            k0 = pl.multiple_of(jnp.maximum(q0 - PIECE, 0), PIECE)
            gather = lambda ref, start, n: jnp.concatenate(
                [ref[r, pl.ds(start, n), :] for r in range(NSLAB)], axis=0).astype(BF16)
            q = gather(qf, q0, PIECE)
            k = gather(kf, k0, 2 * PIECE)
            v = gather(vf, k0, 2 * PIECE)
            pv, m, l = block(q, k, v, 2 + first)
            for r in range(NSLAB):
                dst = pl.ds(loc, PIECE)
                src = slice(r * PIECE, (r + 1) * PIECE)
                num[2 * NSLAB + r, dst, :] = pv[src]
                mm[2 * NSLAB + r, dst, :] = m[src]
                ll[2 * NSLAB + r, dst, :] = l[src]
            if (loc + PIECE) % BLK == 0:
                combine(loc // BLK)
        return 0

    lax.fori_loop(0, n_chunks, chunk_body, 0)


def _attention(z3):
    B, S, _ = z3.shape
    rows4 = S // NSLAB
    blk = lambda off: pl.BlockSpec((None, S, HEAD_DIM), lambda b, h, off=off: (b, 0, off + h))
    return pl.pallas_call(
        functools.partial(_attn_kernel, seq=S),
        grid=(B, N_HEADS),
        in_specs=[blk(0), blk(N_HEADS), blk(2 * N_HEADS)],
        out_specs=pl.BlockSpec((None, S, HEAD_DIM), lambda b, h: (b, 0, h)),
        out_shape=jax.ShapeDtypeStruct((B, S, D_ATTN), BF16),
        scratch_shapes=[
            pltpu.VMEM((NSLAB, rows4, HEAD_DIM), F32),
            pltpu.VMEM((NSLAB, rows4, HEAD_DIM), F32),
            pltpu.VMEM((NSLAB, rows4, HEAD_DIM), F32),
            pltpu.VMEM((3 * NSLAB, SLAB_CHUNK, HEAD_DIM), F32),
            pltpu.VMEM((3 * NSLAB, SLAB_CHUNK, HEAD_DIM), F32),
            pltpu.VMEM((3 * NSLAB, SLAB_CHUNK, HEAD_DIM), F32),
            pltpu.VMEM((3, NSLAB * BLK, HEAD_DIM), F32),
            pltpu.VMEM((SLAB_CHUNK // BLK, NSLAB * BLK, HEAD_DIM), F32),
            pltpu.VMEM((4, BLK, 2 * BLK), F32),
        ],
        compiler_params=_cparams(("parallel", "parallel")),
        name="dilated_attn",
    )(z3, z3, z3)


SUBLANES = 8
LANES = 128
HALO = 32
CONV_ROWS = 64
PIECE_COLS = 256
NORM_ROWS = 16


def _fused_in_kernel(*refs, n_tiles, tiles_per_seq, groups_per_step, n_cast):
    x_ref, g_ref, w_ref, wdw_ref, bdw_ref, gln_ref, bln_ref, zeros_ref = refs[:8]
    cast_src = refs[8:8 + n_cast]
    zq_ref, oc_ref = refs[8 + n_cast:10 + n_cast]
    cast_dst = refs[10 + n_cast:10 + 2 * n_cast]
    a_ref, zs, ubuf, yrow = refs[10 + 2 * n_cast:]
    i, j = pl.program_id(0), pl.program_id(1)
    nj = pl.num_programs(1)
    tm, C = zs.shape
    cur, prev = i % 2, (i + 1) % 2
    off = HALO - (CONV_WIDTH - 1)
    win_rows = CONV_ROWS + HALO

    @pl.when((i == 0) & (j == 0))
    def _():
        ubuf[...] = jnp.zeros_like(ubuf)

    @pl.when(j == 0)
    def _():
        a_ref[...] = _rms(x_ref[...], g_ref[...]).astype(BF16)

    n_blocks = groups_per_step * (C // LANES)

    def conv_block(it):
        k = it // (C // LANES)
        c0 = (it % (C // LANES)) * LANES
        if not isinstance(it, int):
            c0 = pl.multiple_of(c0, LANES)
        t0 = pl.multiple_of((j * groups_per_step + k) * CONV_ROWS, CONV_ROWS)
        win = ubuf[prev, pl.ds(t0, win_rows), pl.ds(c0, LANES)]
        acc = jnp.zeros((CONV_ROWS, LANES), F32)
        for s in range(SUBLANES):
            ys = win if s == 0 else pltpu.roll(win, win_rows - s, axis=0)
            for blk8 in range(HALO // SUBLANES + 1):
                tap = SUBLANES * blk8 + s - off
                if 0 <= tap < CONV_WIDTH:
                    acc = acc + (ys[SUBLANES * blk8:SUBLANES * blk8 + CONV_ROWS]
                                 * wdw_ref[pl.ds(tap, 1), pl.ds(c0, LANES)])
        yrow[k, :, pl.ds(c0, LANES)] = acc
        return acc

    def zero_of(value):
        return (pltpu.bitcast(value[:SUBLANES, :LANES], jnp.uint32) & zeros_ref[...]).astype(F32)

    def norm_rows():
        for k in range(groups_per_step):
            t0 = pl.multiple_of((j * groups_per_step + k) * CONV_ROWS, CONV_ROWS)
            for r in range(0, CONV_ROWS, NORM_ROWS):
                y = yrow[k, r:r + NORM_ROWS, :] + bdw_ref[...]
                mu = jnp.mean(y, axis=-1, keepdims=True)
                yc = y - mu
                var = jnp.mean(yc * yc, axis=-1, keepdims=True)
                yn = yc * lax.rsqrt(var + EPS) * gln_ref[...] + bln_ref[...]
                oc_ref[pl.ds(t0 + r, NORM_ROWS), :] = (yn * jax.nn.sigmoid(yn)).astype(oc_ref.dtype)

    @pl.when((j < nj - 1) & (i < n_tiles))
    def _():
        scale = jnp.where(j == 0, Q_SCALE, 1.0)
        lead = 2 * SUBLANES
        n_half = C // 2
        for it in range(n_blocks):
            zero = jnp.tile(zero_of(conv_block(it)).astype(BF16), (lead // SUBLANES, PIECE_COLS // LANES))
            k0, n0 = (it // 2) * PIECE_COLS, (it % 2) * n_half
            lhs = jnp.concatenate([a_ref[:lead, k0:k0 + PIECE_COLS] + zero, a_ref[lead:, k0:k0 + PIECE_COLS]],
                                  axis=0)
            part = jnp.dot(lhs, w_ref[k0:k0 + PIECE_COLS, n0:n0 + n_half], preferred_element_type=F32)
            zs[:, n0:n0 + n_half] = part if it < 2 else zs[:, n0:n0 + n_half] + part
        zq_ref[...] = (zs[...] * scale).astype(zq_ref.dtype)
        norm_rows()

    @pl.when((j < nj - 1) & (i == n_tiles))
    def _():
        def body(it, _):
            conv_block(it)
            return 0

        lax.fori_loop(0, n_blocks, body, 0)
        norm_rows()

    @pl.when((j == nj - 1) & (i < n_tiles))
    def _():
        gate = jnp.dot(a_ref[...], w_ref[...], preferred_element_type=F32)
        first = (i % tiles_per_seq) == 0
        ubuf[cur, pl.ds(0, HALO), :] = jnp.where(first, 0.0, ubuf[prev, pl.ds(tm, HALO), :])
        ubuf[cur, pl.ds(HALO, tm), :] = zs[...] * jax.nn.sigmoid(gate)
        for src, dst in zip(cast_src, cast_dst):
            dst[...] = src[...].astype(dst.dtype)


def _fused_in(x2, g, w, w_dw, b_dw, g_ln, b_ln, seq, cast_weights, tm=512):
    M, D = x2.shape
    N = w.shape[1]
    C = w_dw.shape[1]
    tn = D_ATTN
    nj = N // tn
    n_tiles = M // tm
    groups = tm // CONV_ROWS
    assert C == tn and N == nj * tn and groups % (nj - 1) == 0 and seq % tm == 0
    last = n_tiles - 1
    vec = pl.BlockSpec((1, C), lambda i, j: (0, 0))
    def slab(wt, arrive=0):
        return pl.BlockSpec((wt.shape[0] // n_tiles, wt.shape[1]),
                            lambda i, j: (jnp.clip(i - jnp.where(j < arrive, 1, 0), 0, last), 0))

    arrivals = [1 + k % (nj - 1) for k in range(len(cast_weights))]
    assert all(wt.shape[0] % (16 * n_tiles) == 0 for wt in cast_weights)
    return pl.pallas_call(
        functools.partial(_fused_in_kernel, n_tiles=n_tiles, tiles_per_seq=seq // tm,
                          groups_per_step=groups // (nj - 1), n_cast=len(cast_weights)),
        grid=(n_tiles + 1, nj),
        in_specs=[
            pl.BlockSpec((tm, D), lambda i, j: (jnp.minimum(i, last), 0)),
            pl.BlockSpec((1, D), lambda i, j: (0, 0)),
            pl.BlockSpec((D, tn), lambda i, j: (0, j)),
            pl.BlockSpec((CONV_WIDTH, C), lambda i, j: (0, 0)),
            vec, vec, vec,
            pl.BlockSpec((SUBLANES, LANES), lambda i, j: (0, 0)),
        ] + [slab(wt, s) for wt, s in zip(cast_weights, arrivals)],
        out_specs=[
            pl.BlockSpec((tm, tn), lambda i, j: (jnp.minimum(i, last),
                                                 jnp.where(i > last, nj - 2, jnp.minimum(j, nj - 2)))),
            pl.BlockSpec((tm, C), lambda i, j: (jnp.maximum(i - 1, 0), 0)),
        ] + [slab(wt) for wt in cast_weights],
        out_shape=[jax.ShapeDtypeStruct((M, (nj - 1) * tn), BF16), jax.ShapeDtypeStruct((M, C), BF16)]
        + [jax.ShapeDtypeStruct(wt.shape, BF16) for wt in cast_weights],
        scratch_shapes=[
            pltpu.VMEM((tm, D), BF16),
            pltpu.VMEM((tm, C), F32),
            pltpu.VMEM((2, HALO + tm, C), F32),
            pltpu.VMEM((groups // (nj - 1), CONV_ROWS, C), F32),
        ],
        compiler_params=_cparams(("arbitrary", "arbitrary")),
        name="in_proj_conv",
    )(x2, g, w, w_dw, b_dw, g_ln, b_ln, jnp.zeros((SUBLANES, LANES), jnp.uint32), *cast_weights)


def _out_proj_kernel(oa_ref, oc_ref, x_ref, wa_ref, wc_ref, g_ref, h_ref, f_ref):
    acc = jnp.dot(oa_ref[...], wa_ref[...], preferred_element_type=F32)
    acc = acc + jnp.dot(oc_ref[...], wc_ref[...], preferred_element_type=F32)
    h = x_ref[...] + acc
    h_ref[...] = h
    f_ref[...] = _rms(h, g_ref[...]).astype(f_ref.dtype)


def _out_proj(oa, oc, x2, w_out, g, tm=512):
    M, D = x2.shape
    Ka, Kc = oa.shape[1], oc.shape[1]
    row = lambda n: pl.BlockSpec((tm, n), lambda i: (i, 0))
    return pl.pallas_call(
        _out_proj_kernel,
        grid=(M // tm,),
        in_specs=[
            row(Ka), row(Kc), row(D),
            pl.BlockSpec((Ka, D), lambda i: (0, 0)),
            pl.BlockSpec((Kc, D), lambda i: (Ka // Kc, 0)),
            pl.BlockSpec((1, D), lambda i: (0, 0)),
        ],
        out_specs=[row(D), row(D)],
        out_shape=[jax.ShapeDtypeStruct((M, D), F32), jax.ShapeDtypeStruct((M, D), BF16)],
        compiler_params=_cparams(("parallel",)),
        name="out_proj",
    )(oa, oc, x2, w_out, w_out, g)


def _ffn_kernel(f_ref, wg_ref, wu_ref, wd_ref, o_ref):
    @pl.when(pl.program_id(1) == 0)
    def _():
        o_ref[...] = jnp.zeros_like(o_ref)

    f = f_ref[...]
    g = jnp.dot(f, wg_ref[...], preferred_element_type=F32)
    u = jnp.dot(f, wu_ref[...], preferred_element_type=F32)
    act = (g * jax.nn.sigmoid(g) * u).astype(BF16)
    o_ref[...] += jnp.dot(act, wd_ref[...], preferred_element_type=F32)


def _ffn(f, wg, wu, wd, tm=1024, tf=512):
    M, D = f.shape
    F = wg.shape[1]
    return pl.pallas_call(
        _ffn_kernel,
        grid=(M // tm, F // tf),
        in_specs=[
            pl.BlockSpec((tm, D), lambda i, j: (i, 0)),
            pl.BlockSpec((D, tf), lambda i, j: (0, j)),
            pl.BlockSpec((D, tf), lambda i, j: (0, j)),
            pl.BlockSpec((tf, D), lambda i, j: (j, 0)),
        ],
        out_specs=pl.BlockSpec((tm, D), lambda i, j: (i, 0)),
        out_shape=jax.ShapeDtypeStruct((M, D), F32),
        compiler_params=_cparams(("parallel", "arbitrary")),
        name="ffn",
    )(f, wg, wu, wd)


def _ple_kernel(h_ref, d_ref, p_ref, wpg_ref, bpg_ref, wple_ref, gp_ref, gf_ref, o_ref, *, final):
    h2 = h_ref[...] + d_ref[...]
    a = _rms(h2, gp_ref[...]).astype(BF16)
    gte = jax.nn.sigmoid(jnp.dot(a, wpg_ref[...], preferred_element_type=F32) + bpg_ref[...])
    ple = jnp.dot(p_ref[...].astype(BF16), wple_ref[...], preferred_element_type=F32)
    h3 = h2 + ple * gte
    o_ref[...] = _rms(h3, gf_ref[...]) if final else h3


def _ple(h1, d, p2, wpg, bpg, wple, g_ple, g_final, final, tm=512):
    M, D = h1.shape
    P = p2.shape[1]
    row = lambda n: pl.BlockSpec((tm, n), lambda i: (i, 0))
    full = lambda r, c: pl.BlockSpec((r, c), lambda i: (0, 0))
    return pl.pallas_call(
        functools.partial(_ple_kernel, final=final),
        grid=(M // tm,),
        in_specs=[row(D), row(D), row(P), full(D, D), full(1, D), full(P, D), full(1, D), full(1, D)],
        out_specs=row(D),
        out_shape=jax.ShapeDtypeStruct((M, D), F32),
        compiler_params=_cparams(("parallel",)),
        name="ple_final",
    )(h1, d, p2, wpg, bpg, wple, g_ple, g_final)


def kernel(x, p, g_mix, w_in, w_dw, b_dw, g_conv_ln, b_conv_ln, w_out, g_ffn, w_gate, w_up, w_down,
           g_ple, w_pgate, b_pgate, w_ple, g_final):
    B, S, D = x.shape
    depth = w_in.shape[0]
    M = B * S
    h = x.reshape(M, D)
    out = None
    for i in range(depth):
        last = i == depth - 1
        zq, oc, wo, wg, wu, wd, wpg = _fused_in(
            h, g_mix[i][None], w_in[i].astype(BF16), w_dw[i, :, 0, :], b_dw[i][None], g_conv_ln[i][None],
            b_conv_ln[i][None], S, (w_out[i], w_gate[i], w_up[i], w_down[i], w_pgate[i]))
        oa = _attention(zq.reshape(B, S, -1)).reshape(M, -1)
        h1, f = _out_proj(oa, oc, h, wo, g_ffn[i][None])
        d = _ffn(f, wg, wu, wd)
        out = _ple(h1, d, p[i].reshape(M, -1), wpg, b_pgate[i][None],
                   w_ple[i].astype(BF16), g_ple[i][None], g_final[None], last)
        h = out
    return out.reshape(B, S, D)
```

```python
import functools

import jax
import jax.numpy as jnp
from jax import lax
from jax.experimental import pallas as pl
from jax.experimental.pallas import tpu as pltpu

F32 = jnp.float32
BF16 = jnp.bfloat16

EPS = 1e-6
N_HEADS = 8
HEAD_DIM = 128
D_ATTN = N_HEADS * HEAD_DIM
BLK = 128
DILATIONS = (1, 4, 16)
CONV_WIDTH = 31
CHUNK = DILATIONS[-1] * BLK
NEG = -1e30
Q_SCALE = HEAD_DIM ** -0.5 * 1.4426950408889634

VMEM_LIMIT = 56 * 1024 * 1024


def _cparams(sem):
    return pltpu.CompilerParams(dimension_semantics=sem, vmem_limit_bytes=VMEM_LIMIT)


def _rms(x, g):
    return x * lax.rsqrt(jnp.mean(x * x, axis=-1, keepdims=True) + EPS) * g


NSLAB = 4
SLAB_CHUNK = CHUNK // NSLAB
PIECE = BLK // NSLAB


def _attn_kernel(q_ref, k_ref, v_ref, o_ref, qf, kf, vf, num, mm, ll, stage, onat, bias, *, seq):
    cp = NSLAB * BLK
    n_chunks = seq // CHUNK

    def cast_piece(i):
        t0 = pl.multiple_of(i * cp, cp)
        s0 = pl.multiple_of(i * BLK, BLK)
        for j, (src, dst) in enumerate(((q_ref, qf), (k_ref, kf), (v_ref, vf))):
            stage[j] = src[pl.ds(t0, cp), :].astype(F32)
            for r in range(NSLAB):
                dst[r, pl.ds(s0, BLK), :] = stage[j, pl.ds(r, BLK, stride=NSLAB), :]

    def cast_body(i, _):
        cast_piece(i)
        return 0

    lax.fori_loop(0, CHUNK // cp, cast_body, 0)

    qi = lax.broadcasted_iota(jnp.int32, (BLK, 2 * BLK), 0)
    kj = lax.broadcasted_iota(jnp.int32, (BLK, 2 * BLK), 1)

    def put(slot, dist):
        bias[slot] = jnp.where((dist >= 0) & (dist <= BLK), 0.0, NEG).astype(F32)

    put(0, qi + BLK - kj)
    put(1, qi - kj)
    d1 = NSLAB * (qi % PIECE - kj % (2 * PIECE)) + (qi // PIECE - kj // (2 * PIECE))
    put(2, d1 + BLK)
    put(3, d1)

    ones = jnp.ones((2 * BLK, HEAD_DIM), BF16)

    def block(q, k, v, bias_slot):
        s = lax.dot_general(q, k, (((1,), (1,)), ((), ())), preferred_element_type=F32)
        s = s + bias[bias_slot]
        m = jnp.max(s, axis=-1, keepdims=True)
        e = jnp.exp2(s - m).astype(BF16)
        pv = jnp.dot(e, jnp.concatenate([v, ones], axis=1), preferred_element_type=F32)
        return pv[:, :HEAD_DIM], jnp.broadcast_to(m, (BLK, HEAD_DIM)), pv[:, HEAD_DIM:]

    def chunk_body(c, _):
        r0 = pl.multiple_of(c * SLAB_CHUNK, SLAB_CHUNK)
        first_chunk = jnp.where(c == 0, 1, 0)

        nxt = jnp.minimum(c + 1, n_chunks - 1)
        for p in range(CHUNK // cp):
            cast_piece(nxt * (CHUNK // cp) + p)

        for mres in range(NSLAB):
            k0 = jnp.maximum(r0 - SLAB_CHUNK, 0) + mres
            for r in range(NSLAB):
                q = qf[r, pl.ds(r0 + mres, BLK, stride=NSLAB), :].astype(BF16)
                k = kf[r, pl.ds(k0, 2 * BLK, stride=NSLAB), :].astype(BF16)
                v = vf[r, pl.ds(k0, 2 * BLK, stride=NSLAB), :].astype(BF16)
                pv, m, l = block(q, k, v, first_chunk)
                dst = pl.ds(mres, BLK, stride=NSLAB)
                num[r, dst, :] = pv
                mm[r, dst, :] = m
                ll[r, dst, :] = l

        for nb in range(SLAB_CHUNK // BLK):
            loc = nb * BLK
            q0 = pl.multiple_of(r0 + loc, BLK)
            first = jnp.where(q0 == 0, 1, 0)
            k0 = pl.multiple_of(jnp.maximum(q0 - BLK, 0), BLK)
            for r in range(NSLAB):
                q = qf[r, pl.ds(q0, BLK), :].astype(BF16)
                k = kf[r, pl.ds(k0, 2 * BLK), :].astype(BF16)
                v = vf[r, pl.ds(k0, 2 * BLK), :].astype(BF16)
                pv, m, l = block(q, k, v, first)
                dst = pl.ds(loc, BLK)
                num[NSLAB + r, dst, :] = pv
                mm[NSLAB + r, dst, :] = m
                ll[NSLAB + r, dst, :] = l

        def combine(g):
            rows = pl.ds(g * BLK, BLK)
            for r in range(NSLAB):
                m0, m1, m2 = mm[r, rows, :], mm[NSLAB + r, rows, :], mm[2 * NSLAB + r, rows, :]
                mx = jnp.maximum(jnp.maximum(m0, m1), m2)
                a0, a1, a2 = jnp.exp2(m0 - mx), jnp.exp2(m1 - mx), jnp.exp2(m2 - mx)
                den = a0 * ll[r, rows, :] + a1 * ll[NSLAB + r, rows, :] + a2 * ll[2 * NSLAB + r, rows, :]
                o = (a0 * num[r, rows, :] + a1 * num[NSLAB + r, rows, :]
                     + a2 * num[2 * NSLAB + r, rows, :]) / den
                onat[g, pl.ds(r, BLK, stride=NSLAB), :] = o
            t0 = pl.multiple_of(c * CHUNK + g * (NSLAB * BLK), NSLAB * BLK)
            o_ref[pl.ds(t0, NSLAB * BLK), :] = onat[g].astype(o_ref.dtype)

        for nb in range(SLAB_CHUNK // PIECE):
            loc = nb * PIECE
            q0 = pl.multiple_of(r0 + loc, PIECE)
            first = jnp.where(q0 == 0, 1, 0)
            k0 = pl.multiple_of(jnp.maximum(q0 - PIECE, 0), PIECE)
            gather = lambda ref, start, n: jnp.concatenate(
                [ref[r, pl.ds(start, n), :] for r in range(NSLAB)], axis=0).astype(BF16)
            q = gather(qf, q0, PIECE)
            k = gather(kf, k0, 2 * PIECE)
            v = gather(vf, k0, 2 * PIECE)
            pv, m, l = block(q, k, v, 2 + first)
            for r in range(NSLAB):
                dst = pl.ds(loc, PIECE)
                src = slice(r * PIECE, (r + 1) * PIECE)
                num[2 * NSLAB + r, dst, :] = pv[src]
                mm[2 * NSLAB + r, dst, :] = m[src]
                ll[2 * NSLAB + r, dst, :] = l[src]
            if (loc + PIECE) % BLK == 0:
                combine(loc // BLK)
        return 0

    lax.fori_loop(0, n_chunks, chunk_body, 0)


def _attention(z3):
    B, S, _ = z3.shape
    rows4 = S // NSLAB
    blk = lambda off: pl.BlockSpec((None, S, HEAD_DIM), lambda b, h, off=off: (b, 0, off + h))
    return pl.pallas_call(
        functools.partial(_attn_kernel, seq=S),
        grid=(B, N_HEADS),
        in_specs=[blk(0), blk(N_HEADS), blk(2 * N_HEADS)],
        out_specs=pl.BlockSpec((None, S, HEAD_DIM), lambda b, h: (b, 0, h)),
        out_shape=jax.ShapeDtypeStruct((B, S, D_ATTN), BF16),
        scratch_shapes=[
            pltpu.VMEM((NSLAB, rows4, HEAD_DIM), F32),
            pltpu.VMEM((NSLAB, rows4, HEAD_DIM), F32),
            pltpu.VMEM((NSLAB, rows4, HEAD_DIM), F32),
            pltpu.VMEM((3 * NSLAB, SLAB_CHUNK, HEAD_DIM), F32),
            pltpu.VMEM((3 * NSLAB, SLAB_CHUNK, HEAD_DIM), F32),
            pltpu.VMEM((3 * NSLAB, SLAB_CHUNK, HEAD_DIM), F32),
            pltpu.VMEM((3, NSLAB * BLK, HEAD_DIM), F32),
            pltpu.VMEM((SLAB_CHUNK // BLK, NSLAB * BLK, HEAD_DIM), F32),
            pltpu.VMEM((4, BLK, 2 * BLK), F32),
        ],
        compiler_params=_cparams(("parallel", "parallel")),
        name="dilated_attn",
    )(z3, z3, z3)


SUBLANES = 8
LANES = 128
HALO = 32
CONV_ROWS = 32
PIECE_COLS = 256
NORM_ROWS = 16


def _fused_in_kernel(*refs, n_tiles, tiles_per_seq, groups_per_step, n_cast):
    x_ref, g_ref, w_ref, wdw_ref, bdw_ref, gln_ref, bln_ref, zeros_ref = refs[:8]
    cast_src = refs[8:8 + n_cast]
    zq_ref, oc_ref = refs[8 + n_cast:10 + n_cast]
    cast_dst = refs[10 + n_cast:10 + 2 * n_cast]
    a_ref, zs, ubuf, yrow = refs[10 + 2 * n_cast:]
    i, j = pl.program_id(0), pl.program_id(1)
    nj = pl.num_programs(1)
    tm, C = zs.shape
    cur, prev = i % 2, (i + 1) % 2
    off = HALO - (CONV_WIDTH - 1)
    win_rows = CONV_ROWS + HALO

    @pl.when((i == 0) & (j == 0))
    def _():
        ubuf[...] = jnp.zeros_like(ubuf)

    @pl.when(j == 0)
    def _():
        a_ref[...] = _rms(x_ref[...], g_ref[...]).astype(BF16)

    n_blocks = groups_per_step * (C // LANES)

    def conv_block(it):
        k = it // (C // LANES)
        c0 = (it % (C // LANES)) * LANES
        if not isinstance(it, int):
            c0 = pl.multiple_of(c0, LANES)
        t0 = pl.multiple_of((j * groups_per_step + k) * CONV_ROWS, CONV_ROWS)
        win = ubuf[prev, pl.ds(t0, win_rows), pl.ds(c0, LANES)]
        acc = jnp.zeros((CONV_ROWS, LANES), F32)
        for s in range(SUBLANES):
            ys = win if s == 0 else pltpu.roll(win, win_rows - s, axis=0)
            for blk8 in range(HALO // SUBLANES + 1):
                tap = SUBLANES * blk8 + s - off
                if 0 <= tap < CONV_WIDTH:
                    acc = acc + (ys[SUBLANES * blk8:SUBLANES * blk8 + CONV_ROWS]
                                 * wdw_ref[pl.ds(tap, 1), pl.ds(c0, LANES)])
        yrow[k, :, pl.ds(c0, LANES)] = acc
        return acc

    def zero_of(value):
        return (pltpu.bitcast(value[:SUBLANES, :LANES], jnp.uint32) & zeros_ref[...]).astype(F32)

    def norm_rows():
        for k in range(groups_per_step):
            t0 = pl.multiple_of((j * groups_per_step + k) * CONV_ROWS, CONV_ROWS)
            for r in range(0, CONV_ROWS, NORM_ROWS):
                y = yrow[k, r:r + NORM_ROWS, :] + bdw_ref[...]
                mu = jnp.mean(y, axis=-1, keepdims=True)
                yc = y - mu
                var = jnp.mean(yc * yc, axis=-1, keepdims=True)
                yn = yc * lax.rsqrt(var + EPS) * gln_ref[...] + bln_ref[...]
                oc_ref[pl.ds(t0 + r, NORM_ROWS), :] = (yn * jax.nn.sigmoid(yn)).astype(oc_ref.dtype)

    @pl.when((j < nj - 1) & (i < n_tiles))
    def _():
        scale = jnp.where(j == 0, Q_SCALE, 1.0)
        lead = 2 * SUBLANES
        n_split = n_blocks // (a_ref.shape[1] // PIECE_COLS)
        n_half = C // n_split
        for it in range(n_blocks):
            zero = jnp.tile(zero_of(conv_block(it)).astype(BF16), (lead // SUBLANES, PIECE_COLS // LANES))
            k0, n0 = (it // n_split) * PIECE_COLS, (it % n_split) * n_half
            lhs = jnp.concatenate([a_ref[:lead, k0:k0 + PIECE_COLS] + zero, a_ref[lead:, k0:k0 + PIECE_COLS]],
                                  axis=0)
            part = jnp.dot(lhs, w_ref[k0:k0 + PIECE_COLS, n0:n0 + n_half], preferred_element_type=F32)
            zs[:, n0:n0 + n_half] = part if it < n_split else zs[:, n0:n0 + n_half] + part
        zq_ref[...] = (zs[...] * scale).astype(zq_ref.dtype)
        norm_rows()

    @pl.when((j < nj - 1) & (i == n_tiles))
    def _():
        def body(it, _):
            conv_block(it)
            return 0

        lax.fori_loop(0, n_blocks, body, 0)
        norm_rows()

    @pl.when((j == nj - 1) & (i < n_tiles))
    def _():
        gate = jnp.dot(a_ref[...], w_ref[...], preferred_element_type=F32)
        first = (i % tiles_per_seq) == 0
        ubuf[cur, pl.ds(0, HALO), :] = jnp.where(first, 0.0, ubuf[prev, pl.ds(tm, HALO), :])
        ubuf[cur, pl.ds(HALO, tm), :] = zs[...] * jax.nn.sigmoid(gate)
        for src, dst in zip(cast_src, cast_dst):
            dst[...] = src[...].astype(dst.dtype)


def _fused_in(x2, g, w, w_dw, b_dw, g_ln, b_ln, seq, cast_weights, tm=512):
    M, D = x2.shape
    N = w.shape[1]
    C = w_dw.shape[1]
    tn = D_ATTN
    nj = N // tn
    n_tiles = M // tm
    groups = tm // CONV_ROWS
    assert C == tn and N == nj * tn and groups % (nj - 1) == 0 and seq % tm == 0
    last = n_tiles - 1
    vec = pl.BlockSpec((1, C), lambda i, j: (0, 0))
    def slab(wt, arrive=0):
        return pl.BlockSpec((wt.shape[0] // n_tiles, wt.shape[1]),
                            lambda i, j: (jnp.clip(i - jnp.where(j < arrive, 1, 0), 0, last), 0))

    arrivals = [1 + k % (nj - 1) for k in range(len(cast_weights))]
    assert all(wt.shape[0] % (16 * n_tiles) == 0 for wt in cast_weights)
    return pl.pallas_call(
        functools.partial(_fused_in_kernel, n_tiles=n_tiles, tiles_per_seq=seq // tm,
                          groups_per_step=groups // (nj - 1), n_cast=len(cast_weights)),
        grid=(n_tiles + 1, nj),
        in_specs=[
            pl.BlockSpec((tm, D), lambda i, j: (jnp.minimum(i, last), 0)),
            pl.BlockSpec((1, D), lambda i, j: (0, 0)),
            pl.BlockSpec((D, tn), lambda i, j: (0, j)),
            pl.BlockSpec((CONV_WIDTH, C), lambda i, j: (0, 0)),
            vec, vec, vec,
            pl.BlockSpec((SUBLANES, LANES), lambda i, j: (0, 0)),
        ] + [slab(wt, s) for wt, s in zip(cast_weights, arrivals)],
        out_specs=[
            pl.BlockSpec((tm, tn), lambda i, j: (jnp.minimum(i, last),
                                                 jnp.where(i > last, nj - 2, jnp.minimum(j, nj - 2)))),
            pl.BlockSpec((tm, C), lambda i, j: (jnp.maximum(i - 1, 0), 0)),
        ] + [slab(wt) for wt in cast_weights],
        out_shape=[jax.ShapeDtypeStruct((M, (nj - 1) * tn), BF16), jax.ShapeDtypeStruct((M, C), BF16)]
        + [jax.ShapeDtypeStruct(wt.shape, BF16) for wt in cast_weights],
        scratch_shapes=[
            pltpu.VMEM((tm, D), BF16),
            pltpu.VMEM((tm, C), F32),
            pltpu.VMEM((2, HALO + tm, C), F32),
            pltpu.VMEM((groups // (nj - 1), CONV_ROWS, C), F32),
        ],
        compiler_params=_cparams(("arbitrary", "arbitrary")),
        name="in_proj_conv",
    )(x2, g, w, w_dw, b_dw, g_ln, b_ln, jnp.zeros((SUBLANES, LANES), jnp.uint32), *cast_weights)


def _out_proj_kernel(oa_ref, oc_ref, x_ref, wa_ref, wc_ref, g_ref, h_ref, f_ref):
    acc = jnp.dot(oa_ref[...], wa_ref[...], preferred_element_type=F32)
    acc = acc + jnp.dot(oc_ref[...], wc_ref[...], preferred_element_type=F32)
    h = x_ref[...] + acc
    h_ref[...] = h
    f_ref[...] = _rms(h, g_ref[...]).astype(f_ref.dtype)


def _out_proj(oa, oc, x2, w_out, g, tm=512):
    M, D = x2.shape
    Ka, Kc = oa.shape[1], oc.shape[1]
    row = lambda n: pl.BlockSpec((tm, n), lambda i: (i, 0))
    return pl.pallas_call(
        _out_proj_kernel,
        grid=(M // tm,),
        in_specs=[
            row(Ka), row(Kc), row(D),
            pl.BlockSpec((Ka, D), lambda i: (0, 0)),
            pl.BlockSpec((Kc, D), lambda i: (Ka // Kc, 0)),
            pl.BlockSpec((1, D), lambda i: (0, 0)),
        ],
        out_specs=[row(D), row(D)],
        out_shape=[jax.ShapeDtypeStruct((M, D), F32), jax.ShapeDtypeStruct((M, D), BF16)],
        compiler_params=_cparams(("parallel",)),
        name="out_proj",
    )(oa, oc, x2, w_out, w_out, g)


def _ffn_kernel(f_ref, wg_ref, wu_ref, wd_ref, o_ref):
    @pl.when(pl.program_id(1) == 0)
    def _():
        o_ref[...] = jnp.zeros_like(o_ref)

    f = f_ref[...]
    g = jnp.dot(f, wg_ref[...], preferred_element_type=F32)
    u = jnp.dot(f, wu_ref[...], preferred_element_type=F32)
    act = (g * jax.nn.sigmoid(g) * u).astype(BF16)
    o_ref[...] += jnp.dot(act, wd_ref[...], preferred_element_type=F32)


def _ffn(f, wg, wu, wd, tm=1024, tf=512):
    M, D = f.shape
    F = wg.shape[1]
    return pl.pallas_call(
        _ffn_kernel,
        grid=(M // tm, F // tf),
        in_specs=[
            pl.BlockSpec((tm, D), lambda i, j: (i, 0)),
            pl.BlockSpec((D, tf), lambda i, j: (0, j)),
            pl.BlockSpec((D, tf), lambda i, j: (0, j)),
            pl.BlockSpec((tf, D), lambda i, j: (j, 0)),
        ],
        out_specs=pl.BlockSpec((tm, D), lambda i, j: (i, 0)),
        out_shape=jax.ShapeDtypeStruct((M, D), F32),
        compiler_params=_cparams(("parallel", "arbitrary")),
        name="ffn",
    )(f, wg, wu, wd)


def _ple_kernel(h_ref, d_ref, p_ref, wpg_ref, bpg_ref, wple_ref, gp_ref, gf_ref, o_ref, *, final):
    h2 = h_ref[...] + d_ref[...]
    a = _rms(h2, gp_ref[...]).astype(BF16)
    gte = jax.nn.sigmoid(jnp.dot(a, wpg_ref[...], preferred_element_type=F32) + bpg_ref[...])
    ple = jnp.dot(p_ref[...].astype(BF16), wple_ref[...], preferred_element_type=F32)
    h3 = h2 + ple * gte
    o_ref[...] = _rms(h3, gf_ref[...]) if final else h3


def _ple(h1, d, p2, wpg, bpg, wple, g_ple, g_final, final, tm=512):
    M, D = h1.shape
    P = p2.shape[1]
    row = lambda n: pl.BlockSpec((tm, n), lambda i: (i, 0))
    full = lambda r, c: pl.BlockSpec((r, c), lambda i: (0, 0))
    return pl.pallas_call(
        functools.partial(_ple_kernel, final=final),
        grid=(M // tm,),
        in_specs=[row(D), row(D), row(P), full(D, D), full(1, D), full(P, D), full(1, D), full(1, D)],
        out_specs=row(D),
        out_shape=jax.ShapeDtypeStruct((M, D), F32),
        compiler_params=_cparams(("parallel",)),
        name="ple_final",
    )(h1, d, p2, wpg, bpg, wple, g_ple, g_final)


def kernel(x, p, g_mix, w_in, w_dw, b_dw, g_conv_ln, b_conv_ln, w_out, g_ffn, w_gate, w_up, w_down,
           g_ple, w_pgate, b_pgate, w_ple, g_final):
    B, S, D = x.shape
    depth = w_in.shape[0]
    M = B * S
    h = x.reshape(M, D)
    out = None
    for i in range(depth):
        last = i == depth - 1
        zq, oc, wo, wg, wu, wd, wpg = _fused_in(
            h, g_mix[i][None], w_in[i].astype(BF16), w_dw[i, :, 0, :], b_dw[i][None], g_conv_ln[i][None],
            b_conv_ln[i][None], S, (w_out[i], w_gate[i], w_up[i], w_down[i], w_pgate[i]))
        oa = _attention(zq.reshape(B, S, -1)).reshape(M, -1)
        h1, f = _out_proj(oa, oc, h, wo, g_ffn[i][None])
        d = _ffn(f, wg, wu, wd)
        out = _ple(h1, d, p[i].reshape(M, -1), wpg, b_pgate[i][None],
                   w_ple[i].astype(BF16), g_ple[i][None], g_final[None], last)
        h = out
    return out.reshape(B, S, D)
```

```python
import functools

import jax
import jax.numpy as jnp
from jax import lax
from jax.experimental import pallas as pl
from jax.experimental.pallas import tpu as pltpu

F32 = jnp.float32
BF16 = jnp.bfloat16

EPS = 1e-6
N_HEADS = 8
HEAD_DIM = 128
D_ATTN = N_HEADS * HEAD_DIM
BLK = 128
DILATIONS = (1, 4, 16)
CONV_WIDTH = 31
CHUNK = DILATIONS[-1] * BLK
NEG = -1e30
Q_SCALE = HEAD_DIM ** -0.5 * 1.4426950408889634

VMEM_LIMIT = 62 * 1024 * 1024


def _cparams(sem):
    return pltpu.CompilerParams(dimension_semantics=sem, vmem_limit_bytes=VMEM_LIMIT)


def _rms(x, g):
    return x * lax.rsqrt(jnp.mean(x * x, axis=-1, keepdims=True) + EPS) * g


NSLAB = 4
SLAB_CHUNK = CHUNK // NSLAB
PIECE = BLK // NSLAB


def _attn_kernel(q_ref, k_ref, v_ref, o_ref, qf, kf, vf, num, mm, ll, stage, onat, bias, *, seq):
    cp = NSLAB * BLK
    n_chunks = seq // CHUNK

    def cast_piece(i):
        t0 = pl.multiple_of(i * cp, cp)
        s0 = pl.multiple_of(i * BLK, BLK)
        for j, (src, dst) in enumerate(((q_ref, qf), (k_ref, kf), (v_ref, vf))):
            stage[j] = src[pl.ds(t0, cp), :].astype(F32)
            for r in range(NSLAB):
                dst[r, pl.ds(s0, BLK), :] = stage[j, pl.ds(r, BLK, stride=NSLAB), :]

    def cast_body(i, _):
        cast_piece(i)
        return 0

    lax.fori_loop(0, CHUNK // cp, cast_body, 0)

    qi = lax.broadcasted_iota(jnp.int32, (BLK, 2 * BLK), 0)
    kj = lax.broadcasted_iota(jnp.int32, (BLK, 2 * BLK), 1)

    def put(slot, dist):
        bias[slot] = jnp.where((dist >= 0) & (dist <= BLK), 0.0, NEG).astype(F32)

    put(0, qi + BLK - kj)
    put(1, qi - kj)
    d1 = NSLAB * (qi % PIECE - kj % (2 * PIECE)) + (qi // PIECE - kj // (2 * PIECE))
    put(2, d1 + BLK)
    put(3, d1)

    ones = jnp.ones((2 * BLK, HEAD_DIM), BF16)

    def block(q, k, v, bias_slot):
        s = lax.dot_general(q, k, (((1,), (1,)), ((), ())), preferred_element_type=F32)
        s = s + bias[bias_slot]
        m = jnp.max(s, axis=-1, keepdims=True)
        e = jnp.exp2(s - m).astype(BF16)
        pv = jnp.dot(e, jnp.concatenate([v, ones], axis=1), preferred_element_type=F32)
        return pv[:, :HEAD_DIM], jnp.broadcast_to(m, (BLK, HEAD_DIM)), pv[:, HEAD_DIM:]

    def chunk_body(c, _):
        r0 = pl.multiple_of(c * SLAB_CHUNK, SLAB_CHUNK)
        first_chunk = jnp.where(c == 0, 1, 0)

        nxt = jnp.minimum(c + 1, n_chunks - 1)
        for p in range(CHUNK // cp):
            cast_piece(nxt * (CHUNK // cp) + p)

        for mres in range(NSLAB):
            k0 = jnp.maximum(r0 - SLAB_CHUNK, 0) + mres
            for r in range(NSLAB):
                q = qf[r, pl.ds(r0 + mres, BLK, stride=NSLAB), :].astype(BF16)
                k = kf[r, pl.ds(k0, 2 * BLK, stride=NSLAB), :].astype(BF16)
                v = vf[r, pl.ds(k0, 2 * BLK, stride=NSLAB), :].astype(BF16)
                pv, m, l = block(q, k, v, first_chunk)
                dst = pl.ds(mres, BLK, stride=NSLAB)
                num[r, dst, :] = pv
                mm[r, dst, :] = m
                ll[r, dst, :] = l

        for nb in range(SLAB_CHUNK // BLK):
            loc = nb * BLK
            q0 = pl.multiple_of(r0 + loc, BLK)
            first = jnp.where(q0 == 0, 1, 0)
            k0 = pl.multiple_of(jnp.maximum(q0 - BLK, 0), BLK)
            for r in range(NSLAB):
                q = qf[r, pl.ds(q0, BLK), :].astype(BF16)
                k = kf[r, pl.ds(k0, 2 * BLK), :].astype(BF16)
                v = vf[r, pl.ds(k0, 2 * BLK), :].astype(BF16)
                pv, m, l = block(q, k, v, first)
                dst = pl.ds(loc, BLK)
                num[NSLAB + r, dst, :] = pv
                mm[NSLAB + r, dst, :] = m
                ll[NSLAB + r, dst, :] = l

        def combine(g):
            rows = pl.ds(g * BLK, BLK)
            for r in range(NSLAB):
                m0, m1, m2 = mm[r, rows, :], mm[NSLAB + r, rows, :], mm[2 * NSLAB + r, rows, :]
                mx = jnp.maximum(jnp.maximum(m0, m1), m2)
                a0, a1, a2 = jnp.exp2(m0 - mx), jnp.exp2(m1 - mx), jnp.exp2(m2 - mx)
                den = a0 * ll[r, rows, :] + a1 * ll[NSLAB + r, rows, :] + a2 * ll[2 * NSLAB + r, rows, :]
                o = (a0 * num[r, rows, :] + a1 * num[NSLAB + r, rows, :]
                     + a2 * num[2 * NSLAB + r, rows, :]) / den
                onat[g, pl.ds(r, BLK, stride=NSLAB), :] = o
            t0 = pl.multiple_of(c * CHUNK + g * (NSLAB * BLK), NSLAB * BLK)
            o_ref[pl.ds(t0, NSLAB * BLK), :] = onat[g].astype(o_ref.dtype)

        for nb in range(SLAB_CHUNK // PIECE):
            loc = nb * PIECE
            q0 = pl.multiple_of(r0 + loc, PIECE)
            first = jnp.where(q0 == 0, 1, 0)
            k0 = pl.multiple_of(jnp.maximum(q0 - PIECE, 0), PIECE)
            gather = lambda ref, start, n: jnp.concatenate(
                [ref[r, pl.ds(start, n), :] for r in range(NSLAB)], axis=0).astype(BF16)
            q = gather(qf, q0, PIECE)
            k = gather(kf, k0, 2 * PIECE)
            v = gather(vf, k0, 2 * PIECE)
            pv, m, l = block(q, k, v, 2 + first)
            for r in range(NSLAB):
                dst = pl.ds(loc, PIECE)
                src = slice(r * PIECE, (r + 1) * PIECE)
                num[2 * NSLAB + r, dst, :] = pv[src]
                mm[2 * NSLAB + r, dst, :] = m[src]
                ll[2 * NSLAB + r, dst, :] = l[src]
            if (loc + PIECE) % BLK == 0:
                combine(loc // BLK)
        return 0

    lax.fori_loop(0, n_chunks, chunk_body, 0)


def _attention(z3):
    B, S, _ = z3.shape
    rows4 = S // NSLAB
    blk = lambda off: pl.BlockSpec((None, S, HEAD_DIM), lambda b, h, off=off: (b, 0, off + h))
    return pl.pallas_call(
        functools.partial(_attn_kernel, seq=S),
        grid=(B, N_HEADS),
        in_specs=[blk(0), blk(N_HEADS), blk(2 * N_HEADS)],
        out_specs=pl.BlockSpec((None, S, HEAD_DIM), lambda b, h: (b, 0, h)),
        out_shape=jax.ShapeDtypeStruct((B, S, D_ATTN), BF16),
        scratch_shapes=[
            pltpu.VMEM((NSLAB, rows4, HEAD_DIM), F32),
            pltpu.VMEM((NSLAB, rows4, HEAD_DIM), F32),
            pltpu.VMEM((NSLAB, rows4, HEAD_DIM), F32),
            pltpu.VMEM((3 * NSLAB, SLAB_CHUNK, HEAD_DIM), F32),
            pltpu.VMEM((3 * NSLAB, SLAB_CHUNK, HEAD_DIM), F32),
            pltpu.VMEM((3 * NSLAB, SLAB_CHUNK, HEAD_DIM), F32),
            pltpu.VMEM((3, NSLAB * BLK, HEAD_DIM), F32),
            pltpu.VMEM((SLAB_CHUNK // BLK, NSLAB * BLK, HEAD_DIM), F32),
            pltpu.VMEM((4, BLK, 2 * BLK), F32),
        ],
        compiler_params=_cparams(("parallel", "parallel")),
        name="dilated_attn",
    )(z3, z3, z3)


SUBLANES = 8
LANES = 128
HALO = 32
CONV_ROWS = 64
PIECE_COLS = 256
NORM_ROWS = 16


def _fused_in_kernel(*refs, n_tiles, tiles_per_seq, groups_per_step, n_cast):
    x_ref, g_ref, w_ref, wdw_ref, bdw_ref, gln_ref, bln_ref, zeros_ref = refs[:8]
    cast_src = refs[8:8 + n_cast]
    zq_ref, oc_ref = refs[8 + n_cast:10 + n_cast]
    cast_dst = refs[10 + n_cast:10 + 2 * n_cast]
    a_ref, zs, ubuf, yrow = refs[10 + 2 * n_cast:]
    i, j = pl.program_id(0), pl.program_id(1)
    nj = pl.num_programs(1)
    tm, tn = zs.shape
    C = ubuf.shape[2]
    cur, prev = i % 2, (i + 1) % 2
    off = HALO - (CONV_WIDTH - 1)
    win_rows = CONV_ROWS + HALO

    @pl.when((i == 0) & (j == 0))
    def _():
        ubuf[...] = jnp.zeros_like(ubuf)

    @pl.when(j == 0)
    def _():
        a_ref[...] = _rms(x_ref[...], g_ref[...]).astype(BF16)

    n_blocks = groups_per_step * (C // LANES)

    def conv_block(it):
        k = it // (C // LANES)
        c0 = (it % (C // LANES)) * LANES
        if not isinstance(it, int):
            c0 = pl.multiple_of(c0, LANES)
        t0 = pl.multiple_of((j * groups_per_step + k) * CONV_ROWS, CONV_ROWS)
        win = ubuf[prev, pl.ds(t0, win_rows), pl.ds(c0, LANES)]
        acc = jnp.zeros((CONV_ROWS, LANES), F32)
        for s in range(SUBLANES):
            ys = win if s == 0 else pltpu.roll(win, win_rows - s, axis=0)
            for blk8 in range(HALO // SUBLANES + 1):
                tap = SUBLANES * blk8 + s - off
                if 0 <= tap < CONV_WIDTH:
                    acc = acc + (ys[SUBLANES * blk8:SUBLANES * blk8 + CONV_ROWS]
                                 * wdw_ref[pl.ds(tap, 1), pl.ds(c0, LANES)])
        yrow[k, :, pl.ds(c0, LANES)] = acc
        return acc

    def zero_of(value):
        return (pltpu.bitcast(value[:SUBLANES, :LANES], jnp.uint32) & zeros_ref[...]).astype(F32)

    def norm_rows():
        for k in range(groups_per_step):
            t0 = pl.multiple_of((j * groups_per_step + k) * CONV_ROWS, CONV_ROWS)
            for r in range(0, CONV_ROWS, NORM_ROWS):
                y = yrow[k, r:r + NORM_ROWS, :] + bdw_ref[...]
                mu = jnp.mean(y, axis=-1, keepdims=True)
                yc = y - mu
                var = jnp.mean(yc * yc, axis=-1, keepdims=True)
                yn = yc * lax.rsqrt(var + EPS) * gln_ref[...] + bln_ref[...]
                oc_ref[pl.ds(t0 + r, NORM_ROWS), :] = (yn * jax.nn.sigmoid(yn)).astype(oc_ref.dtype)

    @pl.when((j < nj - 1) & (i < n_tiles))
    def _():
        col = lax.broadcasted_iota(jnp.int32, (1, tn), 1)
        scale = jnp.where((j == 0) & (col < D_ATTN), Q_SCALE, 1.0)
        lead = 2 * SUBLANES
        n_split = n_blocks // (a_ref.shape[1] // PIECE_COLS)
        n_half = tn // n_split
        for it in range(n_blocks):
            zero = jnp.tile(zero_of(conv_block(it)).astype(BF16), (lead // SUBLANES, PIECE_COLS // LANES))
            k0, n0 = (it // n_split) * PIECE_COLS, (it % n_split) * n_half
            lhs = jnp.concatenate([a_ref[:lead, k0:k0 + PIECE_COLS] + zero, a_ref[lead:, k0:k0 + PIECE_COLS]],
                                  axis=0)
            part = jnp.dot(lhs, w_ref[k0:k0 + PIECE_COLS, n0:n0 + n_half], preferred_element_type=F32)
            zs[:, n0:n0 + n_half] = part if it < n_split else zs[:, n0:n0 + n_half] + part
        zq_ref[...] = (zs[...] * scale).astype(zq_ref.dtype)
        norm_rows()

    @pl.when((j < nj - 1) & (i == n_tiles))
    def _():
        def body(it, _):
            conv_block(it)
            return 0

        lax.fori_loop(0, n_blocks, body, 0)
        norm_rows()

    @pl.when((j == nj - 1) & (i < n_tiles))
    def _():
        gate = jnp.dot(a_ref[...], w_ref[:, :C], preferred_element_type=F32)
        first = (i % tiles_per_seq) == 0
        ubuf[cur, pl.ds(0, HALO), :] = jnp.where(first, 0.0, ubuf[prev, pl.ds(tm, HALO), :])
        ubuf[cur, pl.ds(HALO, tm), :] = zs[:, tn - C:] * jax.nn.sigmoid(gate)
        for src, dst in zip(cast_src, cast_dst):
            dst[...] = src[...].astype(dst.dtype)


def _fused_in(x2, g, w, w_dw, b_dw, g_ln, b_ln, seq, cast_weights, tm=512):
    M, D = x2.shape
    N = w.shape[1]
    C = w_dw.shape[1]
    tn = 2 * C
    nj = N // tn
    n_tiles = M // tm
    groups = tm // CONV_ROWS
    assert C == D_ATTN and N == nj * tn and groups % (nj - 1) == 0 and seq % tm == 0
    last = n_tiles - 1
    vec = pl.BlockSpec((1, C), lambda i, j: (0, 0))
    def slab(wt, arrive=0):
        return pl.BlockSpec((wt.shape[0] // n_tiles, wt.shape[1]),
                            lambda i, j: (jnp.clip(i - jnp.where(j < arrive, 1, 0), 0, last), 0))

    arrivals = [1 + k % (nj - 1) for k in range(len(cast_weights))]
    assert all(wt.shape[0] % (16 * n_tiles) == 0 for wt in cast_weights)
    return pl.pallas_call(
        functools.partial(_fused_in_kernel, n_tiles=n_tiles, tiles_per_seq=seq // tm,
                          groups_per_step=groups // (nj - 1), n_cast=len(cast_weights)),
        grid=(n_tiles + 1, nj),
        in_specs=[
            pl.BlockSpec((tm, D), lambda i, j: (jnp.minimum(i, last), 0)),
            pl.BlockSpec((1, D), lambda i, j: (0, 0)),
            pl.BlockSpec((D, tn), lambda i, j: (0, j)),
            pl.BlockSpec((CONV_WIDTH, C), lambda i, j: (0, 0)),
            vec, vec, vec,
            pl.BlockSpec((SUBLANES, LANES), lambda i, j: (0, 0)),
        ] + [slab(wt, s) for wt, s in zip(cast_weights, arrivals)],
        out_specs=[
            pl.BlockSpec((tm, tn), lambda i, j: (jnp.minimum(i, last),
                                                 jnp.where(i > last, nj - 2, jnp.minimum(j, nj - 2)))),
            pl.BlockSpec((tm, C), lambda i, j: (jnp.maximum(i - 1, 0), 0)),
        ] + [slab(wt) for wt in cast_weights],
        out_shape=[jax.ShapeDtypeStruct((M, (nj - 1) * tn), BF16), jax.ShapeDtypeStruct((M, C), BF16)]
        + [jax.ShapeDtypeStruct(wt.shape, BF16) for wt in cast_weights],
        scratch_shapes=[
            pltpu.VMEM((tm, D), BF16),
            pltpu.VMEM((tm, tn), F32),
            pltpu.VMEM((2, HALO + tm, C), F32),
            pltpu.VMEM((groups // (nj - 1), CONV_ROWS, C), F32),
        ],
        compiler_params=_cparams(("arbitrary", "arbitrary")),
        name="in_proj_conv",
    )(x2, g, w, w_dw, b_dw, g_ln, b_ln, jnp.zeros((SUBLANES, LANES), jnp.uint32), *cast_weights)


def _out_proj_kernel(oa_ref, oc_ref, x_ref, wa_ref, wc_ref, g_ref, h_ref, f_ref):
    acc = jnp.dot(oa_ref[...], wa_ref[...], preferred_element_type=F32)
    acc = acc + jnp.dot(oc_ref[...], wc_ref[...], preferred_element_type=F32)
    h = x_ref[...] + acc
    h_ref[...] = h
    f_ref[...] = _rms(h, g_ref[...]).astype(f_ref.dtype)


def _out_proj(oa, oc, x2, w_out, g, tm=512):
    M, D = x2.shape
    Ka, Kc = oa.shape[1], oc.shape[1]
    row = lambda n: pl.BlockSpec((tm, n), lambda i: (i, 0))
    return pl.pallas_call(
        _out_proj_kernel,
        grid=(M // tm,),
        in_specs=[
            row(Ka), row(Kc), row(D),
            pl.BlockSpec((Ka, D), lambda i: (0, 0)),
            pl.BlockSpec((Kc, D), lambda i: (Ka // Kc, 0)),
            pl.BlockSpec((1, D), lambda i: (0, 0)),
        ],
        out_specs=[row(D), row(D)],
        out_shape=[jax.ShapeDtypeStruct((M, D), F32), jax.ShapeDtypeStruct((M, D), BF16)],
        compiler_params=_cparams(("parallel",)),
        name="out_proj",
    )(oa, oc, x2, w_out, w_out, g)


def _ffn_kernel(f_ref, wg_ref, wu_ref, wd_ref, o_ref):
    @pl.when(pl.program_id(1) == 0)
    def _():
        o_ref[...] = jnp.zeros_like(o_ref)

    f = f_ref[...]
    g = jnp.dot(f, wg_ref[...], preferred_element_type=F32)
    u = jnp.dot(f, wu_ref[...], preferred_element_type=F32)
    act = (g * jax.nn.sigmoid(g) * u).astype(BF16)
    o_ref[...] += jnp.dot(act, wd_ref[...], preferred_element_type=F32)


def _ffn(f, wg, wu, wd, tm=1024, tf=512):
    M, D = f.shape
    F = wg.shape[1]
    return pl.pallas_call(
        _ffn_kernel,
        grid=(M // tm, F // tf),
        in_specs=[
            pl.BlockSpec((tm, D), lambda i, j: (i, 0)),
            pl.BlockSpec((D, tf), lambda i, j: (0, j)),
            pl.BlockSpec((D, tf), lambda i, j: (0, j)),
            pl.BlockSpec((tf, D), lambda i, j: (j, 0)),
        ],
        out_specs=pl.BlockSpec((tm, D), lambda i, j: (i, 0)),
        out_shape=jax.ShapeDtypeStruct((M, D), F32),
        compiler_params=_cparams(("parallel", "arbitrary")),
        name="ffn",
    )(f, wg, wu, wd)


def _ple_kernel(h_ref, d_ref, p_ref, wpg_ref, bpg_ref, wple_ref, gp_ref, gf_ref, o_ref, *, final):
    h2 = h_ref[...] + d_ref[...]
    a = _rms(h2, gp_ref[...]).astype(BF16)
    gte = jax.nn.sigmoid(jnp.dot(a, wpg_ref[...], preferred_element_type=F32) + bpg_ref[...])
    ple = jnp.dot(p_ref[...].astype(BF16), wple_ref[...], preferred_element_type=F32)
    h3 = h2 + ple * gte
    o_ref[...] = _rms(h3, gf_ref[...]) if final else h3


def _ple(h1, d, p2, wpg, bpg, wple, g_ple, g_final, final, tm=512):
    M, D = h1.shape
    P = p2.shape[1]
    row = lambda n: pl.BlockSpec((tm, n), lambda i: (i, 0))
    full = lambda r, c: pl.BlockSpec((r, c), lambda i: (0, 0))
    return pl.pallas_call(
        functools.partial(_ple_kernel, final=final),
        grid=(M // tm,),
        in_specs=[row(D), row(D), row(P), full(D, D), full(1, D), full(P, D), full(1, D), full(1, D)],
        out_specs=row(D),
        out_shape=jax.ShapeDtypeStruct((M, D), F32),
        compiler_params=_cparams(("parallel",)),
        name="ple_final",
    )(h1, d, p2, wpg, bpg, wple, g_ple, g_final)


def _pad_cols(w, multiple):
    return jnp.pad(w, ((0, 0), (0, -w.shape[1] % multiple)))


def kernel(x, p, g_mix, w_in, w_dw, b_dw, g_conv_ln, b_conv_ln, w_out, g_ffn, w_gate, w_up, w_down,
           g_ple, w_pgate, b_pgate, w_ple, g_final):
    B, S, D = x.shape
    depth = w_in.shape[0]
    M = B * S
    h = x.reshape(M, D)
    out = None
    for i in range(depth):
        last = i == depth - 1
        zq, oc, wo, wg, wu, wd, wpg = _fused_in(
            h, g_mix[i][None], _pad_cols(w_in[i].astype(BF16), 2 * D_ATTN), w_dw[i, :, 0, :], b_dw[i][None],
            g_conv_ln[i][None],
            b_conv_ln[i][None], S, (w_out[i], w_gate[i], w_up[i], w_down[i], w_pgate[i]))
        oa = _attention(zq.reshape(B, S, -1)).reshape(M, -1)
        h1, f = _out_proj(oa, oc, h, wo, g_ffn[i][None])
        d = _ffn(f, wg, wu, wd)
        out = _ple(h1, d, p[i].reshape(M, -1), wpg, b_pgate[i][None],
                   w_ple[i].astype(BF16), g_ple[i][None], g_final[None], last)
        h = out
    return out.reshape(B, S, D)
```

```python
import functools

import jax
import jax.numpy as jnp
from jax import lax
from jax.experimental import pallas as pl
from jax.experimental.pallas import tpu as pltpu

F32 = jnp.float32
BF16 = jnp.bfloat16

EPS = 1e-6
N_HEADS = 8
HEAD_DIM = 128
D_ATTN = N_HEADS * HEAD_DIM
BLK = 128
DILATIONS = (1, 4, 16)
CONV_WIDTH = 31
CHUNK = DILATIONS[-1] * BLK
NEG = -1e30
Q_SCALE = HEAD_DIM ** -0.5 * 1.4426950408889634

VMEM_LIMIT = 56 * 1024 * 1024


def _cparams(sem):
    return pltpu.CompilerParams(dimension_semantics=sem, vmem_limit_bytes=VMEM_LIMIT)


def _rms(x, g):
    return x * lax.rsqrt(jnp.mean(x * x, axis=-1, keepdims=True) + EPS) * g


NSLAB = 4
SLAB_CHUNK = CHUNK // NSLAB
PIECE = BLK // NSLAB


def _attn_kernel(q_ref, k_ref, v_ref, o_ref, qf, kf, vf, num, mm, ll, stage, onat, bias, *, seq):
    cp = NSLAB * BLK
    n_chunks = seq // CHUNK

    def cast_piece(i):
        t0 = pl.multiple_of(i * cp, cp)
        s0 = pl.multiple_of(i * BLK, BLK)
        for j, (src, dst) in enumerate(((q_ref, qf), (k_ref, kf), (v_ref, vf))):
            stage[j] = src[pl.ds(t0, cp), :].astype(F32)
            for r in range(NSLAB):
                dst[r, pl.ds(s0, BLK), :] = stage[j, pl.ds(r, BLK, stride=NSLAB), :]

    def cast_body(i, _):
        cast_piece(i)
        return 0

    lax.fori_loop(0, CHUNK // cp, cast_body, 0)

    qi = lax.broadcasted_iota(jnp.int32, (BLK, 2 * BLK), 0)
    kj = lax.broadcasted_iota(jnp.int32, (BLK, 2 * BLK), 1)

    def put(slot, dist):
        bias[slot] = jnp.where((dist >= 0) & (dist <= BLK), 0.0, NEG).astype(F32)

    put(0, qi + BLK - kj)
    put(1, qi - kj)
    d1 = NSLAB * (qi % PIECE - kj % (2 * PIECE)) + (qi // PIECE - kj // (2 * PIECE))
    put(2, d1 + BLK)
    put(3, d1)

    ones = jnp.ones((2 * BLK, HEAD_DIM), BF16)

    def block(q, k, v, bias_slot):
        s = lax.dot_general(q, k, (((1,), (1,)), ((), ())), preferred_element_type=F32)
        s = s + bias[bias_slot]
        m = jnp.max(s, axis=-1, keepdims=True)
        e = jnp.exp2(s - m).astype(BF16)
        pv = jnp.dot(e, jnp.concatenate([v, ones], axis=1), preferred_element_type=F32)
        return pv[:, :HEAD_DIM], jnp.broadcast_to(m, (BLK, HEAD_DIM)), pv[:, HEAD_DIM:]

    def chunk_body(c, _):
        r0 = pl.multiple_of(c * SLAB_CHUNK, SLAB_CHUNK)
        first_chunk = jnp.where(c == 0, 1, 0)

        nxt = jnp.minimum(c + 1, n_chunks - 1)
        for p in range(CHUNK // cp):
            cast_piece(nxt * (CHUNK // cp) + p)

        for mres in range(NSLAB):
            k0 = jnp.maximum(r0 - SLAB_CHUNK, 0) + mres
            for r in range(NSLAB):
                q = qf[r, pl.ds(r0 + mres, BLK, stride=NSLAB), :].astype(BF16)
                k = kf[r, pl.ds(k0, 2 * BLK, stride=NSLAB), :].astype(BF16)
                v = vf[r, pl.ds(k0, 2 * BLK, stride=NSLAB), :].astype(BF16)
                pv, m, l = block(q, k, v, first_chunk)
                dst = pl.ds(mres, BLK, stride=NSLAB)
                num[r, dst, :] = pv
                mm[r, dst, :] = m
                ll[r, dst, :] = l

        for nb in range(SLAB_CHUNK // BLK):
            loc = nb * BLK
            q0 = pl.multiple_of(r0 + loc, BLK)
            first = jnp.where(q0 == 0, 1, 0)
            k0 = pl.multiple_of(jnp.maximum(q0 - BLK, 0), BLK)
            for r in range(NSLAB):
                q = qf[r, pl.ds(q0, BLK), :].astype(BF16)
                k = kf[r, pl.ds(k0, 2 * BLK), :].astype(BF16)
                v = vf[r, pl.ds(k0, 2 * BLK), :].astype(BF16)
                pv, m, l = block(q, k, v, first)
                dst = pl.ds(loc, BLK)
                num[NSLAB + r, dst, :] = pv
                mm[NSLAB + r, dst, :] = m
                ll[NSLAB + r, dst, :] = l

        def combine(g):
            rows = pl.ds(g * BLK, BLK)
            for r in range(NSLAB):
                m0, m1, m2 = mm[r, rows, :], mm[NSLAB + r, rows, :], mm[2 * NSLAB + r, rows, :]
                mx = jnp.maximum(jnp.maximum(m0, m1), m2)
                a0, a1, a2 = jnp.exp2(m0 - mx), jnp.exp2(m1 - mx), jnp.exp2(m2 - mx)
                den = a0 * ll[r, rows, :] + a1 * ll[NSLAB + r, rows, :] + a2 * ll[2 * NSLAB + r, rows, :]
                o = (a0 * num[r, rows, :] + a1 * num[NSLAB + r, rows, :]
                     + a2 * num[2 * NSLAB + r, rows, :]) / den
                onat[g, pl.ds(r, BLK, stride=NSLAB), :] = o
            t0 = pl.multiple_of(c * CHUNK + g * (NSLAB * BLK), NSLAB * BLK)
            o_ref[pl.ds(t0, NSLAB * BLK), :] = onat[g].astype(o_ref.dtype)

        for nb in range(SLAB_CHUNK // PIECE):
            loc = nb * PIECE
            q0 = pl.multiple_of(r0 + loc, PIECE)
            first = jnp.where(q0 == 0, 1, 0)
            k0 = pl.multiple_of(jnp.maximum(q0 - PIECE, 0), PIECE)
            gather = lambda ref, start, n: jnp.concatenate(
                [ref[r, pl.ds(start, n), :] for r in range(NSLAB)], axis=0).astype(BF16)
            q = gather(qf, q0, PIECE)
            k = gather(kf, k0, 2 * PIECE)
            v = gather(vf, k0, 2 * PIECE)
            pv, m, l = block(q, k, v, 2 + first)
            for r in range(NSLAB):
                dst = pl.ds(loc, PIECE)
                src = slice(r * PIECE, (r + 1) * PIECE)
                num[2 * NSLAB + r, dst, :] = pv[src]
                mm[2 * NSLAB + r, dst, :] = m[src]
                ll[2 * NSLAB + r, dst, :] = l[src]
            if (loc + PIECE) % BLK == 0:
                combine(loc // BLK)
        return 0

    lax.fori_loop(0, n_chunks, chunk_body, 0)


def _attention(z3):
    B, S, _ = z3.shape
    rows4 = S // NSLAB
    blk = lambda off: pl.BlockSpec((None, S, HEAD_DIM), lambda b, h, off=off: (b, 0, off + h))
    return pl.pallas_call(
        functools.partial(_attn_kernel, seq=S),
        grid=(B, N_HEADS),
        in_specs=[blk(0), blk(N_HEADS), blk(2 * N_HEADS)],
        out_specs=pl.BlockSpec((None, S, HEAD_DIM), lambda b, h: (b, 0, h)),
        out_shape=jax.ShapeDtypeStruct((B, S, D_ATTN), BF16),
        scratch_shapes=[
            pltpu.VMEM((NSLAB, rows4, HEAD_DIM), F32),
            pltpu.VMEM((NSLAB, rows4, HEAD_DIM), F32),
            pltpu.VMEM((NSLAB, rows4, HEAD_DIM), F32),
            pltpu.VMEM((3 * NSLAB, SLAB_CHUNK, HEAD_DIM), F32),
            pltpu.VMEM((3 * NSLAB, SLAB_CHUNK, HEAD_DIM), F32),
            pltpu.VMEM((3 * NSLAB, SLAB_CHUNK, HEAD_DIM), F32),
            pltpu.VMEM((3, NSLAB * BLK, HEAD_DIM), F32),
            pltpu.VMEM((SLAB_CHUNK // BLK, NSLAB * BLK, HEAD_DIM), F32),
            pltpu.VMEM((4, BLK, 2 * BLK), F32),
        ],
        compiler_params=_cparams(("parallel", "parallel")),
        name="dilated_attn",
    )(z3, z3, z3)


SUBLANES = 8
LANES = 128
HALO = 32
CONV_ROWS = 64
PIECE_COLS = 256
NORM_ROWS = 16


def _fused_in_kernel(*refs, n_tiles, tiles_per_seq, groups_per_step, n_cast):
    x_ref, g_ref, w_ref, wdw_ref, bdw_ref, gln_ref, bln_ref, zeros_ref = refs[:8]
    cast_src = refs[8:8 + n_cast]
    zq_ref, oc_ref = refs[8 + n_cast:10 + n_cast]
    cast_dst = refs[10 + n_cast:10 + 2 * n_cast]
    a_ref, zs, ubuf, yrow = refs[10 + 2 * n_cast:]
    i, j = pl.program_id(0), pl.program_id(1)
    nj = pl.num_programs(1)
    tm, C = zs.shape
    cur, prev = i % 2, (i + 1) % 2
    off = HALO - (CONV_WIDTH - 1)
    win_rows = CONV_ROWS + HALO

    @pl.when((i == 0) & (j == 0))
    def _():
        ubuf[...] = jnp.zeros_like(ubuf)

    @pl.when(j == 0)
    def _():
        a_ref[...] = _rms(x_ref[...], g_ref[...]).astype(BF16)

    n_blocks = groups_per_step * (C // LANES)

    def conv_block(it):
        k = it // (C // LANES)
        c0 = (it % (C // LANES)) * LANES
        if not isinstance(it, int):
            c0 = pl.multiple_of(c0, LANES)
        t0 = pl.multiple_of((j * groups_per_step + k) * CONV_ROWS, CONV_ROWS)
        win = ubuf[prev, pl.ds(t0, win_rows), pl.ds(c0, LANES)]
        acc = jnp.zeros((CONV_ROWS, LANES), F32)
        for s in range(SUBLANES):
            ys = win if s == 0 else pltpu.roll(win, win_rows - s, axis=0)
            for blk8 in range(HALO // SUBLANES + 1):
                tap = SUBLANES * blk8 + s - off
                if 0 <= tap < CONV_WIDTH:
                    acc = acc + (ys[SUBLANES * blk8:SUBLANES * blk8 + CONV_ROWS]
                                 * wdw_ref[pl.ds(tap, 1), pl.ds(c0, LANES)])
        yrow[k, :, pl.ds(c0, LANES)] = acc
        return acc

    def zero_of(value):
        return (pltpu.bitcast(value[:SUBLANES, :LANES], jnp.uint32) & zeros_ref[...]).astype(F32)

    def norm_rows():
        for k in range(groups_per_step):
            t0 = pl.multiple_of((j * groups_per_step + k) * CONV_ROWS, CONV_ROWS)
            for r in range(0, CONV_ROWS, NORM_ROWS):
                y = yrow[k, r:r + NORM_ROWS, :] + bdw_ref[...]
                mu = jnp.mean(y, axis=-1, keepdims=True)
                yc = y - mu
                var = jnp.mean(yc * yc, axis=-1, keepdims=True)
                yn = yc * lax.rsqrt(var + EPS) * gln_ref[...] + bln_ref[...]
                oc_ref[pl.ds(t0 + r, NORM_ROWS), :] = (yn * jax.nn.sigmoid(yn)).astype(oc_ref.dtype)

    @pl.when((j < nj - 1) & (i < n_tiles))
    def _():
        scale = jnp.where(j == 0, Q_SCALE, 1.0)
        lead = 2 * SUBLANES
        n_half = C // 2
        for it in range(n_blocks):
            zero = jnp.tile(zero_of(conv_block(it)).astype(BF16), (lead // SUBLANES, PIECE_COLS // LANES))
            k0, n0 = (it // 2) * PIECE_COLS, (it % 2) * n_half
            lhs = jnp.concatenate([a_ref[:lead, k0:k0 + PIECE_COLS] + zero, a_ref[lead:, k0:k0 + PIECE_COLS]],
                                  axis=0)
            part = jnp.dot(lhs, w_ref[k0:k0 + PIECE_COLS, n0:n0 + n_half], preferred_element_type=F32)
            zs[:, n0:n0 + n_half] = part if it < 2 else zs[:, n0:n0 + n_half] + part
        zq_ref[...] = (zs[...] * scale).astype(zq_ref.dtype)
        norm_rows()

    @pl.when((j < nj - 1) & (i == n_tiles))
    def _():
        def body(it, _):
            conv_block(it)
            return 0

        lax.fori_loop(0, n_blocks, body, 0)
        norm_rows()

    @pl.when((j == nj - 1) & (i < n_tiles))
    def _():
        gate = jnp.dot(a_ref[...], w_ref[...], preferred_element_type=F32)
        first = (i % tiles_per_seq) == 0
        ubuf[cur, pl.ds(0, HALO), :] = jnp.where(first, 0.0, ubuf[prev, pl.ds(tm, HALO), :])
        ubuf[cur, pl.ds(HALO, tm), :] = zs[...] * jax.nn.sigmoid(gate)
        for src, dst in zip(cast_src, cast_dst):
            dst[...] = src[...].astype(dst.dtype)


def _fused_in(x2, g, w, w_dw, b_dw, g_ln, b_ln, seq, cast_weights, tm=512):
    M, D = x2.shape
    N = w.shape[1]
    C = w_dw.shape[1]
    tn = D_ATTN
    nj = N // tn
    n_tiles = M // tm
    groups = tm // CONV_ROWS
    assert C == tn and N == nj * tn and groups % (nj - 1) == 0 and seq % tm == 0
    last = n_tiles - 1
    vec = pl.BlockSpec((1, C), lambda i, j: (0, 0))
    def slab(wt, arrive=0):
        return pl.BlockSpec((wt.shape[0] // n_tiles, wt.shape[1]),
                            lambda i, j: (jnp.clip(i - jnp.where(j < arrive, 1, 0), 0, last), 0))

    arrivals = [1 + k % (nj - 1) for k in range(len(cast_weights))]
    assert all(wt.shape[0] % (16 * n_tiles) == 0 for wt in cast_weights)
    return pl.pallas_call(
        functools.partial(_fused_in_kernel, n_tiles=n_tiles, tiles_per_seq=seq // tm,
                          groups_per_step=groups // (nj - 1), n_cast=len(cast_weights)),
        grid=(n_tiles + 1, nj),
        in_specs=[
            pl.BlockSpec((tm, D), lambda i, j: (jnp.minimum(i, last), 0)),
            pl.BlockSpec((1, D), lambda i, j: (0, 0)),
            pl.BlockSpec((D, tn), lambda i, j: (0, j)),
            pl.BlockSpec((CONV_WIDTH, C), lambda i, j: (0, 0)),
            vec, vec, vec,
            pl.BlockSpec((SUBLANES, LANES), lambda i, j: (0, 0)),
        ] + [slab(wt, s) for wt, s in zip(cast_weights, arrivals)],
        out_specs=[
            pl.BlockSpec((tm, tn), lambda i, j: (jnp.minimum(i, last),
                                                 jnp.where(i > last, nj - 2, jnp.minimum(j, nj - 2)))),
            pl.BlockSpec((tm, C), lambda i, j: (jnp.maximum(i - 1, 0), 0)),
        ] + [slab(wt) for wt in cast_weights],
        out_shape=[jax.ShapeDtypeStruct((M, (nj - 1) * tn), BF16), jax.ShapeDtypeStruct((M, C), BF16)]
        + [jax.ShapeDtypeStruct(wt.shape, BF16) for wt in cast_weights],
        scratch_shapes=[
            pltpu.VMEM((tm, D), BF16),
            pltpu.VMEM((tm, C), F32),
            pltpu.VMEM((2, HALO + tm, C), F32),
            pltpu.VMEM((groups // (nj - 1), CONV_ROWS, C), F32),
        ],
        compiler_params=_cparams(("arbitrary", "arbitrary")),
        name="in_proj_conv",
    )(x2, g, w, w_dw, b_dw, g_ln, b_ln, jnp.zeros((SUBLANES, LANES), jnp.uint32), *cast_weights)


def _out_proj_kernel(oa_ref, oc_ref, x_ref, wa_ref, wc_ref, g_ref, h_ref, f_ref):
    acc = jnp.dot(oa_ref[...], wa_ref[...], preferred_element_type=F32)
    acc = acc + jnp.dot(oc_ref[...], wc_ref[...], preferred_element_type=F32)
    h = x_ref[...] + acc
    h_ref[...] = h
    f_ref[...] = _rms(h, g_ref[...]).astype(f_ref.dtype)


def _out_proj(oa, oc, x2, w_out, g, tm=512):
    M, D = x2.shape
    Ka, Kc = oa.shape[1], oc.shape[1]
    row = lambda n: pl.BlockSpec((tm, n), lambda i: (i, 0))
    return pl.pallas_call(
        _out_proj_kernel,
        grid=(M // tm,),
        in_specs=[
            row(Ka), row(Kc), row(D),
            pl.BlockSpec((Ka, D), lambda i: (0, 0)),
            pl.BlockSpec((Kc, D), lambda i: (Ka // Kc, 0)),
            pl.BlockSpec((1, D), lambda i: (0, 0)),
        ],
        out_specs=[row(D), row(D)],
        out_shape=[jax.ShapeDtypeStruct((M, D), F32), jax.ShapeDtypeStruct((M, D), BF16)],
        compiler_params=_cparams(("parallel",)),
        name="out_proj",
    )(oa, oc, x2, w_out, w_out, g)


def _ffn_kernel(f_ref, wg_ref, wu_ref, wd_ref, o_ref):
    @pl.when(pl.program_id(1) == 0)
    def _():
        o_ref[...] = jnp.zeros_like(o_ref)

    f = f_ref[...]
    tf = wg_ref.shape[1]
    acc = None
    for c0 in range(0, tf, tf // 2):
        g = jnp.dot(f, wg_ref[:, c0:c0 + tf // 2], preferred_element_type=F32)
        u = jnp.dot(f, wu_ref[:, c0:c0 + tf // 2], preferred_element_type=F32)
        act = (g * jax.nn.sigmoid(g) * u).astype(BF16)
        part = jnp.dot(act, wd_ref[c0:c0 + tf // 2, :], preferred_element_type=F32)
        acc = part if acc is None else acc + part
    o_ref[...] += acc


def _ffn(f, wg, wu, wd, tm=1024, tf=512):
    M, D = f.shape
    F = wg.shape[1]
    return pl.pallas_call(
        _ffn_kernel,
        grid=(M // tm, F // tf),
        in_specs=[
            pl.BlockSpec((tm, D), lambda i, j: (i, 0)),
            pl.BlockSpec((D, tf), lambda i, j: (0, j)),
            pl.BlockSpec((D, tf), lambda i, j: (0, j)),
            pl.BlockSpec((tf, D), lambda i, j: (j, 0)),
        ],
        out_specs=pl.BlockSpec((tm, D), lambda i, j: (i, 0)),
        out_shape=jax.ShapeDtypeStruct((M, D), F32),
        compiler_params=_cparams(("parallel", "arbitrary")),
        name="ffn",
    )(f, wg, wu, wd)


def _ple_kernel(h_ref, d_ref, p_ref, wpg_ref, bpg_ref, wple_ref, gp_ref, gf_ref, o_ref, *, final):
    h2 = h_ref[...] + d_ref[...]
    a = _rms(h2, gp_ref[...]).astype(BF16)
    gte = jax.nn.sigmoid(jnp.dot(a, wpg_ref[...], preferred_element_type=F32) + bpg_ref[...])
    ple = jnp.dot(p_ref[...].astype(BF16), wple_ref[...], preferred_element_type=F32)
    h3 = h2 + ple * gte
    o_ref[...] = _rms(h3, gf_ref[...]) if final else h3


def _ple(h1, d, p2, wpg, bpg, wple, g_ple, g_final, final, tm=512):
    M, D = h1.shape
    P = p2.shape[1]
    row = lambda n: pl.BlockSpec((tm, n), lambda i: (i, 0))
    full = lambda r, c: pl.BlockSpec((r, c), lambda i: (0, 0))
    return pl.pallas_call(
        functools.partial(_ple_kernel, final=final),
        grid=(M // tm,),
        in_specs=[row(D), row(D), row(P), full(D, D), full(1, D), full(P, D), full(1, D), full(1, D)],
        out_specs=row(D),
        out_shape=jax.ShapeDtypeStruct((M, D), F32),
        compiler_params=_cparams(("parallel",)),
        name="ple_final",
    )(h1, d, p2, wpg, bpg, wple, g_ple, g_final)


def kernel(x, p, g_mix, w_in, w_dw, b_dw, g_conv_ln, b_conv_ln, w_out, g_ffn, w_gate, w_up, w_down,
           g_ple, w_pgate, b_pgate, w_ple, g_final):
    B, S, D = x.shape
    depth = w_in.shape[0]
    M = B * S
    h = x.reshape(M, D)
    out = None
    for i in range(depth):
        last = i == depth - 1
        zq, oc, wo, wg, wu, wd, wpg = _fused_in(
            h, g_mix[i][None], w_in[i].astype(BF16), w_dw[i, :, 0, :], b_dw[i][None], g_conv_ln[i][None],
            b_conv_ln[i][None], S, (w_out[i], w_gate[i], w_up[i], w_down[i], w_pgate[i]))
        oa = _attention(zq.reshape(B, S, -1)).reshape(M, -1)
        h1, f = _out_proj(oa, oc, h, wo, g_ffn[i][None])
        d = _ffn(f, wg, wu, wd)
        out = _ple(h1, d, p[i].reshape(M, -1), wpg, b_pgate[i][None],
                   w_ple[i].astype(BF16), g_ple[i][None], g_final[None], last)
        h = out
    return out.reshape(B, S, D)
```

```python
import functools

import jax
import jax.numpy as jnp
from jax import lax
from jax.experimental import pallas as pl
from jax.experimental.pallas import tpu as pltpu

F32 = jnp.float32
BF16 = jnp.bfloat16

EPS = 1e-6
N_HEADS = 8
HEAD_DIM = 128
D_ATTN = N_HEADS * HEAD_DIM
BLK = 128
DILATIONS = (1, 4, 16)
CONV_WIDTH = 31
CHUNK = DILATIONS[-1] * BLK
NEG = -1e30
Q_SCALE = HEAD_DIM ** -0.5 * 1.4426950408889634

VMEM_LIMIT = 56 * 1024 * 1024


def _cparams(sem):
    return pltpu.CompilerParams(dimension_semantics=sem, vmem_limit_bytes=VMEM_LIMIT)


def _rms(x, g):
    return x * lax.rsqrt(jnp.mean(x * x, axis=-1, keepdims=True) + EPS) * g


NSLAB = 4
SLAB_CHUNK = CHUNK // NSLAB
PIECE = BLK // NSLAB


def _attn_kernel(q_ref, k_ref, v_ref, o_ref, qf, kf, vf, num, mm, ll, stage, onat, bias, *, seq):
    cp = NSLAB * BLK
    n_chunks = seq // CHUNK

    def cast_piece(i):
        t0 = pl.multiple_of(i * cp, cp)
        s0 = pl.multiple_of(i * BLK, BLK)
        for j, (src, dst) in enumerate(((q_ref, qf), (k_ref, kf), (v_ref, vf))):
            stage[j] = src[pl.ds(t0, cp), :].astype(F32)
            for r in range(NSLAB):
                dst[r, pl.ds(s0, BLK), :] = stage[j, pl.ds(r, BLK, stride=NSLAB), :]

    def cast_body(i, _):
        cast_piece(i)
        return 0

    lax.fori_loop(0, CHUNK // cp, cast_body, 0)

    qi = lax.broadcasted_iota(jnp.int32, (BLK, 2 * BLK), 0)
    kj = lax.broadcasted_iota(jnp.int32, (BLK, 2 * BLK), 1)

    def put(slot, dist):
        bias[slot] = jnp.where((dist >= 0) & (dist <= BLK), 0.0, NEG).astype(F32)

    put(0, qi + BLK - kj)
    put(1, qi - kj)
    d1 = NSLAB * (qi % PIECE - kj % (2 * PIECE)) + (qi // PIECE - kj // (2 * PIECE))
    put(2, d1 + BLK)
    put(3, d1)

    ones = jnp.ones((2 * BLK, HEAD_DIM), BF16)

    def block(q, k, v, bias_slot):
        s = lax.dot_general(q, k, (((1,), (1,)), ((), ())), preferred_element_type=F32)
        s = s + bias[bias_slot]
        m = jnp.max(s, axis=-1, keepdims=True)
        e = jnp.exp2(s - m).astype(BF16)
        pv = jnp.dot(e, jnp.concatenate([v, ones], axis=1), preferred_element_type=F32)
        return pv[:, :HEAD_DIM], jnp.broadcast_to(m, (BLK, HEAD_DIM)), pv[:, HEAD_DIM:]

    def chunk_body(c, _):
        r0 = pl.multiple_of(c * SLAB_CHUNK, SLAB_CHUNK)
        first_chunk = jnp.where(c == 0, 1, 0)

        nxt = jnp.minimum(c + 1, n_chunks - 1)
        for p in range(CHUNK // cp):
            cast_piece(nxt * (CHUNK // cp) + p)

        for mres in range(NSLAB):
            k0 = jnp.maximum(r0 - SLAB_CHUNK, 0) + mres
            for r in range(NSLAB):
                q = qf[r, pl.ds(r0 + mres, BLK, stride=NSLAB), :].astype(BF16)
                k = kf[r, pl.ds(k0, 2 * BLK, stride=NSLAB), :].astype(BF16)
                v = vf[r, pl.ds(k0, 2 * BLK, stride=NSLAB), :].astype(BF16)
                pv, m, l = block(q, k, v, first_chunk)
                dst = pl.ds(mres, BLK, stride=NSLAB)
                num[r, dst, :] = pv
                mm[r, dst, :] = m
                ll[r, dst, :] = l

        for nb in range(SLAB_CHUNK // BLK):
            loc = nb * BLK
            q0 = pl.multiple_of(r0 + loc, BLK)
            first = jnp.where(q0 == 0, 1, 0)
            k0 = pl.multiple_of(jnp.maximum(q0 - BLK, 0), BLK)
            for r in range(NSLAB):
                q = qf[r, pl.ds(q0, BLK), :].astype(BF16)
                k = kf[r, pl.ds(k0, 2 * BLK), :].astype(BF16)
                v = vf[r, pl.ds(k0, 2 * BLK), :].astype(BF16)
                pv, m, l = block(q, k, v, first)
                dst = pl.ds(loc, BLK)
                num[NSLAB + r, dst, :] = pv
                mm[NSLAB + r, dst, :] = m
                ll[NSLAB + r, dst, :] = l

        def combine(g):
            rows = pl.ds(g * BLK, BLK)
            for r in range(NSLAB):
                m0, m1, m2 = mm[r, rows, :], mm[NSLAB + r, rows, :], mm[2 * NSLAB + r, rows, :]
                mx = jnp.maximum(jnp.maximum(m0, m1), m2)
                a0, a1, a2 = jnp.exp2(m0 - mx), jnp.exp2(m1 - mx), jnp.exp2(m2 - mx)
                den = a0 * ll[r, rows, :] + a1 * ll[NSLAB + r, rows, :] + a2 * ll[2 * NSLAB + r, rows, :]
                o = (a0 * num[r, rows, :] + a1 * num[NSLAB + r, rows, :]
                     + a2 * num[2 * NSLAB + r, rows, :]) / den
                onat[g, pl.ds(r, BLK, stride=NSLAB), :] = o
            t0 = pl.multiple_of(c * CHUNK + g * (NSLAB * BLK), NSLAB * BLK)
            o_ref[pl.ds(t0, NSLAB * BLK), :] = onat[g].astype(o_ref.dtype)

        for nb in range(SLAB_CHUNK // PIECE):
            loc = nb * PIECE
            q0 = pl.multiple_of(r0 + loc, PIECE)
            first = jnp.where(q0 == 0, 1, 0)
            k0 = pl.multiple_of(jnp.maximum(q0 - PIECE, 0), PIECE)
            gather = lambda ref, start, n: jnp.concatenate(
                [ref[r, pl.ds(start, n), :] for r in range(NSLAB)], axis=0).astype(BF16)
            q = gather(qf, q0, PIECE)
            k = gather(kf, k0, 2 * PIECE)
            v = gather(vf, k0, 2 * PIECE)
            pv, m, l = block(q, k, v, 2 + first)
            for r in range(NSLAB):
                dst = pl.ds(loc, PIECE)
                src = slice(r * PIECE, (r + 1) * PIECE)
                num[2 * NSLAB + r, dst, :] = pv[src]
                mm[2 * NSLAB + r, dst, :] = m[src]
                ll[2 * NSLAB + r, dst, :] = l[src]
            if (loc + PIECE) % BLK == 0:
                combine(loc // BLK)
        return 0

    lax.fori_loop(0, n_chunks, chunk_body, 0)


def _attention(z3):
    B, S, _ = z3.shape
    rows4 = S // NSLAB
    blk = lambda off: pl.BlockSpec((None, S, HEAD_DIM), lambda b, h, off=off: (b, 0, off + h))
    return pl.pallas_call(
        functools.partial(_attn_kernel, seq=S),
        grid=(B, N_HEADS),
        in_specs=[blk(0), blk(N_HEADS), blk(2 * N_HEADS)],
        out_specs=pl.BlockSpec((None, S, HEAD_DIM), lambda b, h: (b, 0, h)),
        out_shape=jax.ShapeDtypeStruct((B, S, D_ATTN), BF16),
        scratch_shapes=[
            pltpu.VMEM((NSLAB, rows4, HEAD_DIM), F32),
            pltpu.VMEM((NSLAB, rows4, HEAD_DIM), F32),
            pltpu.VMEM((NSLAB, rows4, HEAD_DIM), F32),
            pltpu.VMEM((3 * NSLAB, SLAB_CHUNK, HEAD_DIM), F32),
            pltpu.VMEM((3 * NSLAB, SLAB_CHUNK, HEAD_DIM), F32),
            pltpu.VMEM((3 * NSLAB, SLAB_CHUNK, HEAD_DIM), F32),
            pltpu.VMEM((3, NSLAB * BLK, HEAD_DIM), F32),
            pltpu.VMEM((SLAB_CHUNK // BLK, NSLAB * BLK, HEAD_DIM), F32),
            pltpu.VMEM((4, BLK, 2 * BLK), F32),
        ],
        compiler_params=_cparams(("parallel", "parallel")),
        name="dilated_attn",
    )(z3, z3, z3)


SUBLANES = 8
LANES = 128
HALO = 32
CONV_ROWS = 64
PIECE_COLS = 256
NORM_ROWS = 16


def _fused_in_kernel(*refs, n_tiles, tiles_per_seq, groups_per_step, n_cast):
    x_ref, g_ref, w_ref, wdw_ref, bdw_ref, gln_ref, bln_ref, zeros_ref = refs[:8]
    cast_src = refs[8:8 + n_cast]
    zq_ref, oc_ref = refs[8 + n_cast:10 + n_cast]
    cast_dst = refs[10 + n_cast:10 + 2 * n_cast]
    a_ref, zs, ubuf, yrow, wbuf, wsem = refs[10 + 2 * n_cast:]
    i, j = pl.program_id(0), pl.program_id(1)
    nj = pl.num_programs(1)
    tm, C = zs.shape
    cur, prev = i % 2, (i + 1) % 2

    tn_w = wbuf.shape[2]
    step = i * nj + j
    w_steps = n_tiles * nj
    slot = step % 3

    def w_copy(t):
        col = pl.multiple_of((t % nj) * tn_w, tn_w)
        return pltpu.make_async_copy(w_ref.at[:, pl.ds(col, tn_w)], wbuf.at[t % 3], wsem.at[t % 3])

    @pl.when(step == 0)
    def _():
        w_copy(0).start()
        w_copy(1).start()

    @pl.when(step + 2 < w_steps)
    def _():
        w_copy(step + 2).start()

    @pl.when(step < w_steps)
    def _():
        w_copy(step).wait()
    off = HALO - (CONV_WIDTH - 1)
    win_rows = CONV_ROWS + HALO

    @pl.when((i == 0) & (j == 0))
    def _():
        ubuf[...] = jnp.zeros_like(ubuf)

    @pl.when(j == 0)
    def _():
        a_ref[...] = _rms(x_ref[...], g_ref[...]).astype(BF16)

    n_blocks = groups_per_step * (C // LANES)

    def conv_block(it):
        k = it // (C // LANES)
        c0 = (it % (C // LANES)) * LANES
        if not isinstance(it, int):
            c0 = pl.multiple_of(c0, LANES)
        t0 = pl.multiple_of((j * groups_per_step + k) * CONV_ROWS, CONV_ROWS)
        win = ubuf[prev, pl.ds(t0, win_rows), pl.ds(c0, LANES)]
        acc = jnp.zeros((CONV_ROWS, LANES), F32)
        for s in range(SUBLANES):
            ys = win if s == 0 else pltpu.roll(win, win_rows - s, axis=0)
            for blk8 in range(HALO // SUBLANES + 1):
                tap = SUBLANES * blk8 + s - off
                if 0 <= tap < CONV_WIDTH:
                    acc = acc + (ys[SUBLANES * blk8:SUBLANES * blk8 + CONV_ROWS]
                                 * wdw_ref[pl.ds(tap, 1), pl.ds(c0, LANES)])
        yrow[k, :, pl.ds(c0, LANES)] = acc
        return acc

    def zero_of(value):
        return (pltpu.bitcast(value[:SUBLANES, :LANES], jnp.uint32) & zeros_ref[...]).astype(F32)

    def norm_rows():
        for k in range(groups_per_step):
            t0 = pl.multiple_of((j * groups_per_step + k) * CONV_ROWS, CONV_ROWS)
            for r in range(0, CONV_ROWS, NORM_ROWS):
                y = yrow[k, r:r + NORM_ROWS, :] + bdw_ref[...]
                mu = jnp.mean(y, axis=-1, keepdims=True)
                yc = y - mu
                var = jnp.mean(yc * yc, axis=-1, keepdims=True)
                yn = yc * lax.rsqrt(var + EPS) * gln_ref[...] + bln_ref[...]
                oc_ref[pl.ds(t0 + r, NORM_ROWS), :] = (yn * jax.nn.sigmoid(yn)).astype(oc_ref.dtype)

    @pl.when((j < nj - 1) & (i < n_tiles))
    def _():
        scale = jnp.where(j == 0, Q_SCALE, 1.0)
        lead = 2 * SUBLANES
        n_half = C // 2
        for it in range(n_blocks):
            zero = jnp.tile(zero_of(conv_block(it)).astype(BF16), (lead // SUBLANES, PIECE_COLS // LANES))
            k0, n0 = (it // 2) * PIECE_COLS, (it % 2) * n_half
            lhs = jnp.concatenate([a_ref[:lead, k0:k0 + PIECE_COLS] + zero, a_ref[lead:, k0:k0 + PIECE_COLS]],
                                  axis=0)
            part = jnp.dot(lhs, wbuf[slot, k0:k0 + PIECE_COLS, n0:n0 + n_half], preferred_element_type=F32)
            zs[:, n0:n0 + n_half] = part if it < 2 else zs[:, n0:n0 + n_half] + part
        zq_ref[...] = (zs[...] * scale).astype(zq_ref.dtype)
        norm_rows()

    @pl.when((j < nj - 1) & (i == n_tiles))
    def _():
        def body(it, _):
            conv_block(it)
            return 0

        lax.fori_loop(0, n_blocks, body, 0)
        norm_rows()

    @pl.when((j == nj - 1) & (i < n_tiles))
    def _():
        gate = jnp.dot(a_ref[...], wbuf[slot], preferred_element_type=F32)
        first = (i % tiles_per_seq) == 0
        ubuf[cur, pl.ds(0, HALO), :] = jnp.where(first, 0.0, ubuf[prev, pl.ds(tm, HALO), :])
        ubuf[cur, pl.ds(HALO, tm), :] = zs[...] * jax.nn.sigmoid(gate)
        for src, dst in zip(cast_src, cast_dst):
            dst[...] = src[...].astype(dst.dtype)


def _fused_in(x2, g, w, w_dw, b_dw, g_ln, b_ln, seq, cast_weights, tm=512):
    M, D = x2.shape
    N = w.shape[1]
    C = w_dw.shape[1]
    tn = D_ATTN
    nj = N // tn
    n_tiles = M // tm
    groups = tm // CONV_ROWS
    assert C == tn and N == nj * tn and groups % (nj - 1) == 0 and seq % tm == 0
    last = n_tiles - 1
    vec = pl.BlockSpec((1, C), lambda i, j: (0, 0))
    def slab(wt, arrive=0):
        return pl.BlockSpec((wt.shape[0] // n_tiles, wt.shape[1]),
                            lambda i, j: (jnp.clip(i - jnp.where(j < arrive, 1, 0), 0, last), 0))

    arrivals = [1 + k % (nj - 1) for k in range(len(cast_weights))]
    assert all(wt.shape[0] % (16 * n_tiles) == 0 for wt in cast_weights)
    return pl.pallas_call(
        functools.partial(_fused_in_kernel, n_tiles=n_tiles, tiles_per_seq=seq // tm,
                          groups_per_step=groups // (nj - 1), n_cast=len(cast_weights)),
        grid=(n_tiles + 1, nj),
        in_specs=[
            pl.BlockSpec((tm, D), lambda i, j: (jnp.minimum(i, last), 0)),
            pl.BlockSpec((1, D), lambda i, j: (0, 0)),
            pl.BlockSpec(memory_space=pl.ANY),
            pl.BlockSpec((CONV_WIDTH, C), lambda i, j: (0, 0)),
            vec, vec, vec,
            pl.BlockSpec((SUBLANES, LANES), lambda i, j: (0, 0)),
        ] + [slab(wt, s) for wt, s in zip(cast_weights, arrivals)],
        out_specs=[
            pl.BlockSpec((tm, tn), lambda i, j: (jnp.minimum(i, last),
                                                 jnp.where(i > last, nj - 2, jnp.minimum(j, nj - 2)))),
            pl.BlockSpec((tm, C), lambda i, j: (jnp.maximum(i - 1, 0), 0)),
        ] + [slab(wt) for wt in cast_weights],
        out_shape=[jax.ShapeDtypeStruct((M, (nj - 1) * tn), BF16), jax.ShapeDtypeStruct((M, C), BF16)]
        + [jax.ShapeDtypeStruct(wt.shape, BF16) for wt in cast_weights],
        scratch_shapes=[
            pltpu.VMEM((tm, D), BF16),
            pltpu.VMEM((tm, C), F32),
            pltpu.VMEM((2, HALO + tm, C), F32),
            pltpu.VMEM((groups // (nj - 1), CONV_ROWS, C), F32),
            pltpu.VMEM((3, D, tn), BF16),
            pltpu.SemaphoreType.DMA((3,)),
        ],
        compiler_params=_cparams(("arbitrary", "arbitrary")),
        name="in_proj_conv",
    )(x2, g, w, w_dw, b_dw, g_ln, b_ln, jnp.zeros((SUBLANES, LANES), jnp.uint32), *cast_weights)


def _out_proj_kernel(oa_ref, oc_ref, x_ref, wa_ref, wc_ref, g_ref, h_ref, f_ref):
    acc = jnp.dot(oa_ref[...], wa_ref[...], preferred_element_type=F32)
    acc = acc + jnp.dot(oc_ref[...], wc_ref[...], preferred_element_type=F32)
    h = x_ref[...] + acc
    h_ref[...] = h
    f_ref[...] = _rms(h, g_ref[...]).astype(f_ref.dtype)


def _out_proj(oa, oc, x2, w_out, g, tm=512):
    M, D = x2.shape
    Ka, Kc = oa.shape[1], oc.shape[1]
    row = lambda n: pl.BlockSpec((tm, n), lambda i: (i, 0))
    return pl.pallas_call(
        _out_proj_kernel,
        grid=(M // tm,),
        in_specs=[
            row(Ka), row(Kc), row(D),
            pl.BlockSpec((Ka, D), lambda i: (0, 0)),
            pl.BlockSpec((Kc, D), lambda i: (Ka // Kc, 0)),
            pl.BlockSpec((1, D), lambda i: (0, 0)),
        ],
        out_specs=[row(D), row(D)],
        out_shape=[jax.ShapeDtypeStruct((M, D), F32), jax.ShapeDtypeStruct((M, D), BF16)],
        compiler_params=_cparams(("parallel",)),
        name="out_proj",
    )(oa, oc, x2, w_out, w_out, g)


def _ffn_kernel(f_ref, wg_ref, wu_ref, wd_ref, o_ref):
    @pl.when(pl.program_id(1) == 0)
    def _():
        o_ref[...] = jnp.zeros_like(o_ref)

    f = f_ref[...]
    tf = wg_ref.shape[1]
    acc = None
    for c0 in range(0, tf, tf // 2):
        g = jnp.dot(f, wg_ref[:, c0:c0 + tf // 2], preferred_element_type=F32)
        u = jnp.dot(f, wu_ref[:, c0:c0 + tf // 2], preferred_element_type=F32)
        act = (g * jax.nn.sigmoid(g) * u).astype(BF16)
        part = jnp.dot(act, wd_ref[c0:c0 + tf // 2, :], preferred_element_type=F32)
        acc = part if acc is None else acc + part
    o_ref[...] += acc


def _ffn(f, wg, wu, wd, tm=1024, tf=512):
    M, D = f.shape
    F = wg.shape[1]
    return pl.pallas_call(
        _ffn_kernel,
        grid=(M // tm, F // tf),
        in_specs=[
            pl.BlockSpec((tm, D), lambda i, j: (i, 0)),
            pl.BlockSpec((D, tf), lambda i, j: (0, j)),
            pl.BlockSpec((D, tf), lambda i, j: (0, j)),
            pl.BlockSpec((tf, D), lambda i, j: (j, 0)),
        ],
        out_specs=pl.BlockSpec((tm, D), lambda i, j: (i, 0)),
        out_shape=jax.ShapeDtypeStruct((M, D), F32),
        compiler_params=_cparams(("parallel", "arbitrary")),
        name="ffn",
    )(f, wg, wu, wd)


def _ple_kernel(h_ref, d_ref, p_ref, wpg_ref, bpg_ref, wple_ref, gp_ref, gf_ref, o_ref, *, final):
    h2 = h_ref[...] + d_ref[...]
    a = _rms(h2, gp_ref[...]).astype(BF16)
    gte = jax.nn.sigmoid(jnp.dot(a, wpg_ref[...], preferred_element_type=F32) + bpg_ref[...])
    ple = jnp.dot(p_ref[...].astype(BF16), wple_ref[...], preferred_element_type=F32)
    h3 = h2 + ple * gte
    o_ref[...] = _rms(h3, gf_ref[...]) if final else h3


def _ple(h1, d, p2, wpg, bpg, wple, g_ple, g_final, final, tm=512):
    M, D = h1.shape
    P = p2.shape[1]
    row = lambda n: pl.BlockSpec((tm, n), lambda i: (i, 0))
    full = lambda r, c: pl.BlockSpec((r, c), lambda i: (0, 0))
    return pl.pallas_call(
        functools.partial(_ple_kernel, final=final),
        grid=(M // tm,),
        in_specs=[row(D), row(D), row(P), full(D, D), full(1, D), full(P, D), full(1, D), full(1, D)],
        out_specs=row(D),
        out_shape=jax.ShapeDtypeStruct((M, D), F32),
        compiler_params=_cparams(("parallel",)),
        name="ple_final",
    )(h1, d, p2, wpg, bpg, wple, g_ple, g_final)


def kernel(x, p, g_mix, w_in, w_dw, b_dw, g_conv_ln, b_conv_ln, w_out, g_ffn, w_gate, w_up, w_down,
           g_ple, w_pgate, b_pgate, w_ple, g_final):
    B, S, D = x.shape
    depth = w_in.shape[0]
    M = B * S
    h = x.reshape(M, D)
    out = None
    for i in range(depth):
        last = i == depth - 1
        zq, oc, wo, wg, wu, wd, wpg = _fused_in(
            h, g_mix[i][None], w_in[i].astype(BF16), w_dw[i, :, 0, :], b_dw[i][None], g_conv_ln[i][None],
            b_conv_ln[i][None], S, (w_out[i], w_gate[i], w_up[i], w_down[i], w_pgate[i]))
        oa = _attention(zq.reshape(B, S, -1)).reshape(M, -1)
        h1, f = _out_proj(oa, oc, h, wo, g_ffn[i][None])
        d = _ffn(f, wg, wu, wd)
        out = _ple(h1, d, p[i].reshape(M, -1), wpg, b_pgate[i][None],
                   w_ple[i].astype(BF16), g_ple[i][None], g_final[None], last)
        h = out
    return out.reshape(B, S, D)
```
